```python
import jax, jax.numpy as jnp
from jax import lax
import numpy as np

D_MODEL = 4096
BATCH = 2
SEQ = 8192
DEPTH = 4

CHUNK = 64
Q_BLOCK = 128
LN_EPS = 1e-5
RMS_EPS = 1e-6
ALPHA = (2 * DEPTH) ** 0.25
BETA = (8 * DEPTH) ** -0.25

A_WIDTH = D_MODEL // 2
A_HEAD_DIM = 128
A_HEADS = A_WIDTH // A_HEAD_DIM
A_Q_RANK = 768
A_KV_RANK = 512
IDX_HEADS = 16
IDX_DIM = 64
TOPK_MAX = 256
B_WIDTH = D_MODEL // 2
B_HEAD = 64
B_HEADS = B_WIDTH // B_HEAD
DECAY_LORA = 96
AAA_LORA = 96
GATE_LORA = 256
GN_EPS = 64e-5
C_WIDTH = D_MODEL
C_HEAD = 128
C_HEADS = C_WIDTH // C_HEAD
N_GROUPS = 4
EXPERTS_PER_GROUP = 8
N_EXPERTS = N_GROUPS * EXPERTS_PER_GROUP
EXPERT_TOP_K = 2
D_EXPERT = 256
N_EVEN = (DEPTH + 1) // 2
N_ODD = DEPTH // 2
A_SPLITS = (A_Q_RANK, A_KV_RANK, IDX_DIM, IDX_HEADS)
B_SPLITS = (B_WIDTH, B_WIDTH, B_WIDTH, DECAY_LORA, AAA_LORA, GATE_LORA)
A_COLS = sum(A_SPLITS)
B_COLS = sum(B_SPLITS)
EVEN_COLS = A_COLS + B_COLS
ODD_COLS = 4 * C_WIDTH
EVEN_MIX = A_WIDTH + B_WIDTH

kernel_name = 'hybrid_dsa_rwkv7_hgrn2_hmoe_deepnorm'


def _split(a, sizes):
    return jnp.split(a, np.cumsum(sizes)[:-1].tolist(), axis=-1)


def _layer_norm(x, gain, bias):
    xf = x.astype(jnp.float32)
    mu = xf.mean(-1, keepdims=True)
    var = jnp.square(xf - mu).mean(-1, keepdims=True)
    return ((xf - mu) * lax.rsqrt(var + LN_EPS) * gain + bias).astype(x.dtype)


def _rms_norm(x, gain):
    xf = x.astype(jnp.float32)
    y = xf * lax.rsqrt(jnp.square(xf).mean(-1, keepdims=True) + RMS_EPS)
    return (y * gain).astype(x.dtype)


def _alibi_slopes(n):
    start = 2.0 ** (-8.0 / n)
    return jnp.asarray([start ** (h + 1) for h in range(n)], jnp.float32)


def _token_shift(z):
    return jnp.pad(z, ((0, 0), (1, 0), (0, 0)))[:, :-1]


def _mixer_dsa(cols, q_norm, kv_norm, w_uq, w_uk, w_uv, w_idx_q, idx_k_gain, idx_k_bias):
    bsz, seq = cols.shape[0], cols.shape[1]
    q_lat, kv_lat, k_idx, w_idx = _split(cols, A_SPLITS)
    q_lat = _rms_norm(q_lat, q_norm)
    c_kv = _rms_norm(kv_lat, kv_norm)
    q = jnp.einsum('btr,rhd->bthd', q_lat, w_uq)
    q_abs = jnp.einsum('bthd,rhd->bthr', q, w_uk) * (A_HEAD_DIM ** -0.5)
    q_idx = jnp.einsum('btr,rhd->bthd', q_lat, w_idx_q)
    k_idx = _layer_norm(k_idx, idx_k_gain, idx_k_bias)
    w_idx = w_idx * (IDX_HEADS ** -0.5 * IDX_DIM ** -0.5)
    top_k = min(TOPK_MAX, seq // 4)
    n_blk = seq // Q_BLOCK
    pos = jnp.arange(seq, dtype=jnp.int32)
    slopes = _alibi_slopes(A_HEADS)

    def blocks(a):
        return a.reshape(bsz, n_blk, Q_BLOCK, *a.shape[2:]).swapaxes(0, 1)

    def attend_block(blk):
        qi, wi, qa, qpos = blk
        rel = jax.nn.relu(jnp.einsum('bqhd,bsd->bqhs', qi, k_idx))
        score = jnp.einsum('bqhs,bqh->bqs', rel, wi)
        limit = (qpos // CHUNK + 1) * CHUNK
        admissible = pos[None, :] < limit[:, None]
        score = jnp.where(admissible[None], score, -jnp.inf)
        _, idx = lax.top_k(score, top_k)
        valid = idx < limit[None, :, None]
        c_sel = jax.vmap(lambda c, i: c[i])(c_kv, idx)
        logits = jnp.einsum('bqhr,bqkr->bqhk', qa, c_sel).astype(jnp.float32)
        dist = jnp.abs(qpos[None, :, None] - idx).astype(jnp.float32)
        logits = logits - slopes[None, None, :, None] * dist[:, :, None, :]
        logits = jnp.where(valid[:, :, None, :], logits, -jnp.inf)
        p = jax.nn.softmax(logits, axis=-1).astype(c_sel.dtype)
        o_lat = jnp.einsum('bqhk,bqkr->bqhr', p, c_sel)
        return jnp.einsum('bqhr,rhd->bqhd', o_lat, w_uv)

    out = lax.map(attend_block, (blocks(q_idx), blocks(w_idx), blocks(q_abs), pos.reshape(n_blk, Q_BLOCK)))
    return out.swapaxes(0, 1).reshape(bsz, seq, A_WIDTH)


def _rwkv7_scan(r, decay, k, v, kk, a):
    bsz, seq, nh, n = r.shape

    def step(state, inp):
        r_t, w_t, k_t, v_t, kk_t, a_t = inp
        sa = jnp.einsum('bhvk,bhk->bhv', state, -kk_t)
        state = (state * w_t[:, :, None, :]
                 + sa[..., None] * (kk_t * a_t)[:, :, None, :]
                 + v_t[..., None] * k_t[:, :, None, :])
        return state, jnp.einsum('bhvk,bhk->bhv', state, r_t)

    xs = (r.swapaxes(0, 1), decay.swapaxes(0, 1), k.swapaxes(0, 1),
          v.swapaxes(0, 1), kk.swapaxes(0, 1), a.swapaxes(0, 1))
    s0 = jnp.zeros((bsz, nh, n, n), jnp.float32)
    _, y = lax.scan(step, s0, xs)
    return y.swapaxes(0, 1)


def _mixer_rwkv7(cols, mu, w0, w_up, a0, a_up, g_up, k_k, k_a, r_k, ln_gain, ln_bias):
    bsz, seq = cols.shape[0], cols.shape[1]
    mixed = cols + (_token_shift(cols) - cols) * mu
    r, k, v, wd, ad, gd = _split(mixed, B_SPLITS)
    w = -jax.nn.softplus(-(w0 + jnp.tanh(wd) @ w_up)) - 0.5
    decay = jnp.exp(-jnp.exp(w.astype(jnp.float32)))
    a = jax.nn.sigmoid(a0 + ad @ a_up)
    g = jax.nn.sigmoid(gd) @ g_up

    def heads(z):
        return z.reshape(bsz, seq, B_HEADS, B_HEAD)

    kk = heads(k * k_k).astype(jnp.float32)
    kk = kk / jnp.maximum(jnp.sqrt(jnp.sum(kk * kk, axis=-1, keepdims=True)), 1e-12)
    k = k * (1 + (a - 1) * k_a)
    rh, kh, vh, ah = heads(r), heads(k), heads(v), heads(a)
    y = _rwkv7_scan(rh.astype(jnp.float32), heads(decay), kh.astype(jnp.float32),
                    vh.astype(jnp.float32), kk, ah.astype(jnp.float32))
    mean = y.mean(-1, keepdims=True)
    var = jnp.square(y - mean).mean(-1, keepdims=True)
    y = ((y - mean) * lax.rsqrt(var + GN_EPS)).reshape(bsz, seq, B_WIDTH) * ln_gain + ln_bias
    bonus = jnp.sum(rh * kh * r_k, axis=-1, keepdims=True) * vh
    y = (y + bonus.reshape(bsz, seq, B_WIDTH)) * g
    return y.astype(cols.dtype)


def _mixer_hgrn2(cols, lb, norm_gain):
    bsz, seq = cols.shape[0], cols.shape[1]
    n_chunk = seq // CHUNK
    q, f, i, g = _split(cols, (C_WIDTH, C_WIDTH, C_WIDTH, C_WIDTH))
    q = jax.nn.silu(q).astype(jnp.float32)
    f = f.astype(jnp.float32)
    lb = lb.astype(jnp.float32)
    log_f = jnp.logaddexp(jnp.log(lb), jnp.log1p(-lb) + jax.nn.log_sigmoid(f))
    key = (1 - lb) * jax.nn.sigmoid(-f)

    def chunks(z):
        return z.reshape(bsz, n_chunk, CHUNK, C_HEADS, C_HEAD).swapaxes(0, 1)

    b_cum = jnp.cumsum(chunks(log_f), axis=2)
    causal = jnp.tril(jnp.ones((CHUNK, CHUNK), bool))

    def step(state, inp):
        q_c, k_c, b_c, v_c = inp
        diff = b_c[:, :, None] - b_c[:, None, :]
        dec = jnp.exp(jnp.where(causal[None, :, :, None, None], diff, -jnp.inf))
        scores = jnp.einsum('bthd,bshd,btshd->bhts', q_c, k_c, dec)
        o = (jnp.einsum('bhts,bshe->bthe', scores, v_c)
             + jnp.einsum('bthd,bhde->bthe', q_c * jnp.exp(b_c), state))
        b_last = b_c[:, -1]
        state = (jnp.exp(b_last)[..., None] * state
                 + jnp.einsum('bshd,bshe->bhde', k_c * jnp.exp(b_last[:, None] - b_c), v_c))
        return state, o

    s0 = jnp.zeros((bsz, C_HEADS, C_HEAD, C_HEAD), jnp.float32)
    _, o = lax.scan(step, s0, (chunks(q), chunks(key), b_cum, chunks(i.astype(jnp.float32))))
    o = o.swapaxes(0, 1).reshape(bsz, seq, C_HEADS, C_HEAD)
    o = o * lax.rsqrt(jnp.square(o).mean(-1, keepdims=True) + RMS_EPS)
    o = o.reshape(bsz, seq, C_WIDTH) * norm_gain * jax.nn.silu(g.astype(jnp.float32))
    return o.astype(cols.dtype)


def _hier_moe(h, wg, bg, we, be, w_gate, w_up, w_down):
    bsz, seq, d = h.shape
    x2 = h.reshape(-1, d)
    gl = (x2 @ wg + bg).astype(jnp.float32)
    pg = jax.nn.softmax(gl, axis=-1)
    g_idx = jnp.argmax(gl, axis=-1)
    g_p = jnp.take_along_axis(pg, g_idx[:, None], axis=1)[:, 0]
    el = (x2 @ we + be).astype(jnp.float32).reshape(-1, N_GROUPS, EXPERTS_PER_GROUP)
    el_g = jnp.take_along_axis(el, g_idx[:, None, None], axis=1)[:, 0]
    pe = jax.nn.softmax(el_g, axis=-1)
    top_v, top_i = lax.top_k(pe, EXPERT_TOP_K)
    top_v = top_v / jnp.sum(top_v, axis=-1, keepdims=True)
    gates = g_p[:, None] * top_v
    eid = g_idx[:, None] * EXPERTS_PER_GROUP + top_i
    gate_full = jnp.sum(jax.nn.one_hot(eid, N_EXPERTS, dtype=jnp.float32) * gates[..., None], axis=1)
    hg = jnp.einsum('nd,edf->nef', x2, w_gate)
    hu = jnp.einsum('nd,edf->nef', x2, w_up)
    act = jax.nn.silu(hg) * hu * gate_full[..., None].astype(x2.dtype)
    y = jnp.einsum('nef,efd->nd', act, w_down)
    return y.reshape(bsz, seq, d)


def setup_inputs(seed: int = 0) -> dict:
    key = jax.random.key(seed)
    ks = iter(jax.random.split(key, 40))

    def nrm(shape, scale):
        return jax.random.normal(next(ks), shape, jnp.float32) * scale

    def gain(shape):
        return 1.0 + nrm(shape, 0.02)

    d = D_MODEL
    return {
        'x': nrm((BATCH, SEQ, d), 1.0),
        'ln1_gain': gain((DEPTH, d)),
        'ln1_bias': nrm((DEPTH, d), 0.02),
        'ln2_gain': gain((DEPTH, d)),
        'ln2_bias': nrm((DEPTH, d), 0.02),
        'w_in_even': nrm((N_EVEN, d, EVEN_COLS), d ** -0.5),
        'a_q_norm': gain((N_EVEN, A_Q_RANK)),
        'a_kv_norm': gain((N_EVEN, A_KV_RANK)),
        'a_w_uq': nrm((N_EVEN, A_Q_RANK, A_HEADS, A_HEAD_DIM), A_Q_RANK ** -0.5),
        'a_w_uk': nrm((N_EVEN, A_KV_RANK, A_HEADS, A_HEAD_DIM), A_KV_RANK ** -0.5),
        'a_w_uv': nrm((N_EVEN, A_KV_RANK, A_HEADS, A_HEAD_DIM), A_KV_RANK ** -0.5),
        'a_w_idx_q': nrm((N_EVEN, A_Q_RANK, IDX_HEADS, IDX_DIM), A_Q_RANK ** -0.5),
        'a_idx_k_gain': gain((N_EVEN, IDX_DIM)),
        'a_idx_k_bias': nrm((N_EVEN, IDX_DIM), 0.02),
        'b_mu': jax.random.uniform(next(ks), (N_EVEN, B_COLS), jnp.float32, 0.05, 0.95),
        'b_w0': -2.0 + nrm((N_EVEN, B_WIDTH), 1.0),
        'b_w_up': nrm((N_EVEN, DECAY_LORA, B_WIDTH), 0.5 * DECAY_LORA ** -0.5),
        'b_a0': nrm((N_EVEN, B_WIDTH), 0.1),
        'b_a_up': nrm((N_EVEN, AAA_LORA, B_WIDTH), 0.5 * AAA_LORA ** -0.5),
        'b_g_up': nrm((N_EVEN, GATE_LORA, B_WIDTH), GATE_LORA ** -0.5),
        'b_k_k': 0.85 + nrm((N_EVEN, B_WIDTH), 0.02),
        'b_k_a': 1.0 + nrm((N_EVEN, B_WIDTH), 0.02),
        'b_r_k': nrm((N_EVEN, B_HEADS, B_HEAD), 0.1),
        'b_ln_gain': gain((N_EVEN, B_WIDTH)),
        'b_ln_bias': nrm((N_EVEN, B_WIDTH), 0.02),
        'w_out_even': nrm((N_EVEN, EVEN_MIX, d), BETA * EVEN_MIX ** -0.5),
        'w_in_odd': nrm((N_ODD, d, ODD_COLS), d ** -0.5),
        'c_lb_logits': nrm((N_ODD, C_WIDTH), 0.5),
        'c_norm_gain': gain((N_ODD, C_WIDTH)),
        'w_out_odd': nrm((N_ODD, C_WIDTH, d), BETA * C_WIDTH ** -0.5),
        'router_group': nrm((DEPTH, d, N_GROUPS), d ** -0.5),
        'router_group_bias': nrm((DEPTH, N_GROUPS), 0.01),
        'router_expert': nrm((DEPTH, d, N_EXPERTS), d ** -0.5),
        'router_expert_bias': nrm((DEPTH, N_EXPERTS), 0.01),
        'moe_w_gate': nrm((DEPTH, N_EXPERTS, d, D_EXPERT), d ** -0.5),
        'moe_w_up': nrm((DEPTH, N_EXPERTS, d, D_EXPERT), d ** -0.5),
        'moe_w_down': nrm((DEPTH, N_EXPERTS, D_EXPERT, d), BETA * D_EXPERT ** -0.5),
    }


def reference(x, ln1_gain, ln1_bias, ln2_gain, ln2_bias,
              w_in_even, a_q_norm, a_kv_norm, a_w_uq, a_w_uk, a_w_uv, a_w_idx_q,
              a_idx_k_gain, a_idx_k_bias,
              b_mu, b_w0, b_w_up, b_a0, b_a_up, b_g_up, b_k_k, b_k_a, b_r_k,
              b_ln_gain, b_ln_bias, w_out_even,
              w_in_odd, c_lb_logits, c_norm_gain, w_out_odd,
              router_group, router_group_bias, router_expert, router_expert_bias,
              moe_w_gate, moe_w_up, moe_w_down):
    lb_table = jnp.cumsum(jax.nn.softmax(c_lb_logits.astype(jnp.float32), axis=0), axis=0)
    lb_table = lb_table - lb_table[:1]
    for layer in range(DEPTH):
        j = layer // 2
        if layer % 2 == 0:
            cols = x @ w_in_even[j]
            a_cols, b_cols = jnp.split(cols, [A_COLS], axis=-1)
            y_a = _mixer_dsa(a_cols, a_q_norm[j], a_kv_norm[j], a_w_uq[j], a_w_uk[j], a_w_uv[j],
                             a_w_idx_q[j], a_idx_k_gain[j], a_idx_k_bias[j])
            y_b = _mixer_rwkv7(b_cols, b_mu[j], b_w0[j], b_w_up[j], b_a0[j], b_a_up[j], b_g_up[j],
                               b_k_k[j], b_k_a[j], b_r_k[j], b_ln_gain[j], b_ln_bias[j])
            mix = jnp.concatenate([y_a, y_b], axis=-1) @ w_out_even[j]
        else:
            cols = x @ w_in_odd[j]
            mix = _mixer_hgrn2(cols, lb_table[j], c_norm_gain[j]) @ w_out_odd[j]
        x = _layer_norm(ALPHA * x + mix, ln1_gain[layer], ln1_bias[layer])
        ffn = _hier_moe(x, router_group[layer], router_group_bias[layer], router_expert[layer],
                        router_expert_bias[layer], moe_w_gate[layer], moe_w_up[layer], moe_w_down[layer])
        x = _layer_norm(ALPHA * x + ffn, ln2_gain[layer], ln2_bias[layer])
    return x
```

```python
import functools

import jax
import jax.numpy as jnp
import numpy as np
from jax import lax
from jax.experimental import pallas as pl
from jax.experimental.pallas import tpu as pltpu

F32 = jnp.float32
BF16 = jnp.bfloat16

DEPTH = 4
ALPHA = (2 * DEPTH) ** 0.25
LN_EPS = 1e-5
RMS_EPS = 1e-6
GN_EPS = 64e-5
CHUNK = 64
Q_BLOCK = 128
TOPK_MAX = 256
N_GROUPS = 4
EXPERTS_PER_GROUP = 8
N_EXPERTS = N_GROUPS * EXPERTS_PER_GROUP

LANES = 128
VMEM_LIMIT = 56 * 1024 * 1024


def _cparams(sem):
    return pltpu.CompilerParams(dimension_semantics=sem, vmem_limit_bytes=VMEM_LIMIT)


def _dot(a, b):
    return jnp.dot(a, b, preferred_element_type=F32)


def _dot_nt(a, b):
    return lax.dot_general(a, b, (((1,), (1,)), ((), ())), preferred_element_type=F32)


def _dot_tn(a, b):
    return lax.dot_general(a, b, (((0,), (0,)), ((), ())), preferred_element_type=F32)


def _split_bf16(x):
    hi = x.astype(BF16)
    lo = (x - hi.astype(F32)).astype(BF16)
    return hi, lo


def _mm_body(a_ref, b_ref, o_ref):
    o_ref[...] = _dot(a_ref[...].astype(BF16), b_ref[...]).astype(o_ref.dtype)


def matmul(a, b, out_dtype=F32, tm=512, tn=1024):
    m, k = a.shape
    _, n = b.shape
    tm = min(tm, m)
    tn = min(tn, n)
    assert m % tm == 0 and n % tn == 0
    return pl.pallas_call(
        _mm_body,
        grid=(n // tn, m // tm),
        in_specs=[pl.BlockSpec((tm, k), lambda j, i: (i, 0)),
                  pl.BlockSpec((k, tn), lambda j, i: (0, j))],
        out_specs=pl.BlockSpec((tm, tn), lambda j, i: (i, j)),
        out_shape=jax.ShapeDtypeStruct((m, n), out_dtype),
        compiler_params=_cparams(("parallel", "parallel")),
        name="matmul",
    )(a, b)


def _ln_rows(z, gain, bias):
    mu = jnp.mean(z, axis=-1, keepdims=True)
    zc = z - mu
    var = jnp.mean(zc * zc, axis=-1, keepdims=True)
    return zc * lax.rsqrt(var + LN_EPS) * gain + bias


def _proj_ln_body(a_ref, w_ref, x_ref, g_ref, b_ref, o_ref, ob_ref, acc_ref, *, nk):
    kk = pl.program_id(1)

    @pl.when(kk == 0)
    def _():
        acc_ref[...] = jnp.zeros_like(acc_ref)

    acc_ref[...] += _dot(a_ref[...].astype(BF16), w_ref[...])

    @pl.when(kk == nk - 1)
    def _():
        y = _ln_rows(ALPHA * x_ref[...] + acc_ref[...], g_ref[...], b_ref[...])
        o_ref[...] = y
        ob_ref[...] = y.astype(BF16)


def proj_residual_ln(a, w, x, gain, bias, tm=256, tk=1024):
    m, k = a.shape
    d = w.shape[1]
    nk = k // tk
    return pl.pallas_call(
        functools.partial(_proj_ln_body, nk=nk),
        grid=(m // tm, nk),
        in_specs=[pl.BlockSpec((tm, tk), lambda i, kk: (i, kk)),
                  pl.BlockSpec((tk, d), lambda i, kk: (kk, 0)),
                  pl.BlockSpec((tm, d), lambda i, kk: (i, 0)),
                  pl.BlockSpec((1, d), lambda i, kk: (0, 0)),
                  pl.BlockSpec((1, d), lambda i, kk: (0, 0))],
        out_specs=[pl.BlockSpec((tm, d), lambda i, kk: (i, 0)),
                   pl.BlockSpec((tm, d), lambda i, kk: (i, 0))],
        out_shape=[jax.ShapeDtypeStruct((m, d), F32), jax.ShapeDtypeStruct((m, d), BF16)],
        scratch_shapes=[pltpu.VMEM((tm, d), F32)],
        compiler_params=_cparams(("parallel", "arbitrary")),
        name="proj_residual_ln",
    )(a, w, x, gain.reshape(1, d), bias.reshape(1, d))


def _add_ln_body(x_ref, y0_ref, y1_ref, g_ref, b_ref, o_ref, ob_ref):
    z = ALPHA * x_ref[...] + (y0_ref[...].astype(F32) + y1_ref[...].astype(F32))
    y = _ln_rows(z, g_ref[...], b_ref[...])
    o_ref[...] = y
    ob_ref[...] = y.astype(BF16)


def add_ln(x, y0, y1, gain, bias, tm=256):
    m, d = x.shape
    row = pl.BlockSpec((tm, d), lambda i: (i, 0))
    vec = pl.BlockSpec((1, d), lambda i: (0, 0))
    return pl.pallas_call(
        _add_ln_body,
        grid=(m // tm,),
        in_specs=[row, row, row, vec, vec],
        out_specs=[row, row],
        out_shape=[jax.ShapeDtypeStruct((m, d), F32), jax.ShapeDtypeStruct((m, d), BF16)],
        compiler_params=_cparams(("parallel",)),
        name="add_ln",
    )(x, y0, y1, gain.reshape(1, d), bias.reshape(1, d))


def _router_body(x_ref, wh_ref, wl_ref, b_ref, eid_ref, gate_ref):
    xh, xl = _split_bf16(x_ref[...])
    wh = wh_ref[...]
    logits = _dot(xh, wh) + (_dot(xl, wh) + _dot(xh, wl_ref[...])) + b_ref[...]
    lane = lax.broadcasted_iota(jnp.int32, logits.shape, 1)
    neg = jnp.float32(-jnp.inf)
    is_g = lane < N_GROUPS
    gl = jnp.where(is_g, logits, neg)
    gmax = jnp.max(gl, axis=-1, keepdims=True)
    g_idx = jnp.min(jnp.where(gl == gmax, lane, LANES), axis=-1, keepdims=True)
    g_p = 1.0 / jnp.sum(jnp.exp(gl - gmax), axis=-1, keepdims=True)
    e_lane = lane - N_GROUPS
    in_grp = (e_lane >= g_idx * EXPERTS_PER_GROUP) & (e_lane < (g_idx + 1) * EXPERTS_PER_GROUP)
    el = jnp.where(in_grp, logits, neg)
    emax = jnp.max(el, axis=-1, keepdims=True)
    pe = jnp.exp(el - emax)
    pe = pe / jnp.sum(pe, axis=-1, keepdims=True)
    v1 = jnp.max(pe, axis=-1, keepdims=True)
    i1 = jnp.min(jnp.where(in_grp & (pe == v1), lane, LANES), axis=-1, keepdims=True)
    rest = in_grp & (lane != i1)
    pe2 = jnp.where(rest, pe, -1.0)
    v2 = jnp.max(pe2, axis=-1, keepdims=True)
    i2 = jnp.min(jnp.where(rest & (pe2 == v2), lane, LANES), axis=-1, keepdims=True)
    den = v1 + v2
    eid_ref[...] = jnp.where(lane == 0, i1 - N_GROUPS, i2 - N_GROUPS)
    gate_ref[...] = jnp.where(lane == 0, g_p * (v1 / den), g_p * (v2 / den))


def router(x, wg, bg, we, be, tm=512):
    m, d = x.shape
    w = jnp.zeros((d, LANES), F32).at[:, :N_GROUPS].set(wg).at[:, N_GROUPS:N_GROUPS + N_EXPERTS].set(we)
    b = jnp.zeros((1, LANES), F32).at[0, :N_GROUPS].set(bg).at[0, N_GROUPS:N_GROUPS + N_EXPERTS].set(be)
    wh, wl = _split_bf16(w)
    row = pl.BlockSpec((tm, LANES), lambda i: (i, 0))
    wspec = pl.BlockSpec((d, LANES), lambda i: (0, 0))
    eid, gate = pl.pallas_call(
        _router_body,
        grid=(m // tm,),
        in_specs=[pl.BlockSpec((tm, d), lambda i: (i, 0)), wspec, wspec,
                  pl.BlockSpec((1, LANES), lambda i: (0, 0))],
        out_specs=[row, row],
        out_shape=[jax.ShapeDtypeStruct((m, LANES), jnp.int32), jax.ShapeDtypeStruct((m, LANES), F32)],
        compiler_params=_cparams(("parallel",)),
        name="router",
    )(x, wh, wl, b)
    return eid[:, :2], gate[:, :2]


def _expert_body(te_ref, tv_ref, x_ref, gate_ref, wg_ref, wu_ref, wd_ref, o_ref):
    i = pl.program_id(0)

    @pl.when(tv_ref[i] > 0)
    def _():
        xs = x_ref[...]
        hg = _dot(xs, wg_ref[0])
        hu = _dot(xs, wu_ref[0])
        act = (hg * jax.nn.sigmoid(hg)) * hu * gate_ref[...]
        o_ref[...] = _dot(act.astype(BF16), wd_ref[0]).astype(o_ref.dtype)

    @pl.when(tv_ref[i] == 0)
    def _():
        o_ref[...] = jnp.zeros_like(o_ref)


def expert_ffn(xs, gate_sorted, tile_expert, tile_valid, w_gate, w_up, w_down, tm):
    p, d = xs.shape
    f = w_gate.shape[-1]
    ntiles = p // tm
    grid_spec = pltpu.PrefetchScalarGridSpec(
        num_scalar_prefetch=2,
        grid=(ntiles,),
        in_specs=[pl.BlockSpec((tm, d), lambda i, te, tv: (i, 0)),
                  pl.BlockSpec((tm, 1), lambda i, te, tv: (i, 0)),
                  pl.BlockSpec((1, d, f), lambda i, te, tv: (te[i], 0, 0)),
                  pl.BlockSpec((1, d, f), lambda i, te, tv: (te[i], 0, 0)),
                  pl.BlockSpec((1, f, d), lambda i, te, tv: (te[i], 0, 0))],
        out_specs=pl.BlockSpec((tm, d), lambda i, te, tv: (i, 0)),
    )
    return pl.pallas_call(
        _expert_body,
        grid_spec=grid_spec,
        out_shape=jax.ShapeDtypeStruct((p, d), BF16),
        compiler_params=_cparams(("arbitrary",)),
        name="expert_ffn",
    )(tile_expert, tile_valid, xs, gate_sorted, w_gate, w_up, w_down)


MOE_TILE = 256


def hier_moe(x_f32, x_bf16, wg, bg, we, be, w_gate, w_up, w_down):
    n, d = x_f32.shape
    tm = MOE_TILE
    eid, gates = router(x_f32, wg, bg, we, be)
    e_flat = eid.reshape(-1)
    onehot = (e_flat[:, None] == jnp.arange(N_EXPERTS, dtype=jnp.int32)[None, :]).astype(jnp.int32)
    counts = jnp.sum(onehot, axis=0)
    rank = jnp.sum((jnp.cumsum(onehot, axis=0) - onehot) * onehot, axis=1)
    padded = ((counts + tm - 1) // tm) * tm
    seg_end = jnp.cumsum(padded)
    seg_start = seg_end - padded
    pos = seg_start[e_flat] + rank
    p = 2 * n + N_EXPERTS * tm
    tok = jnp.arange(2 * n, dtype=jnp.int32) // 2
    tok_sorted = jnp.zeros((p,), jnp.int32).at[pos].set(tok)
    gate_sorted = jnp.zeros((p,), F32).at[pos].set(gates.reshape(-1))
    tile_start = jnp.arange(p // tm, dtype=jnp.int32) * tm
    tile_valid = (tile_start < seg_end[-1]).astype(jnp.int32)
    tile_expert = jnp.minimum(jnp.searchsorted(seg_end, tile_start, side="right"), N_EXPERTS - 1)
    last_e = jnp.max(jnp.where(tile_valid > 0, tile_expert, 0))
    tile_expert = jnp.where(tile_valid > 0, tile_expert, last_e).astype(jnp.int32)
    xs = jnp.take(x_bf16, tok_sorted, axis=0)
    ys = expert_ffn(xs, gate_sorted.reshape(p, 1), tile_expert, tile_valid, w_gate, w_up, w_down, tm)
    pos2 = pos.reshape(n, 2)
    return jnp.take(ys, pos2[:, 0], axis=0), jnp.take(ys, pos2[:, 1], axis=0)


C_HEAD = 128
HGRN_LEVELS = (1, 2, 4, 8, 16, 32)


def _cumsum_rows(x, row):
    s = 1
    while s < x.shape[0]:
        x = x + jnp.where(row >= s, pltpu.roll(x, s, axis=0), 0.0)
        s *= 2
    return x


def _anchor_rows(b, m, row):
    n = b.shape[0]
    if m >= 8:
        parts = [jnp.broadcast_to(b[base + m:base + m + 1, :], (2 * m, b.shape[1]))
                 for base in range(0, n, 2 * m)]
        return parts[0] if len(parts) == 1 else jnp.concatenate(parts, axis=0)
    delta = m - (row & (2 * m - 1))
    out = b
    for dlt in range(-(m - 1), m + 1):
        if dlt == 0:
            continue
        out = jnp.where(delta == dlt, pltpu.roll(b, (-dlt) % n, axis=0), out)
    return out


def _hgrn2_body(q_ref, f_ref, i_ref, g_ref, loglb_ref, log1mlb_ref, omlb_ref, gain_ref,
                o_ref, st_ref, *, hb, n_chunk):
    @pl.when(pl.program_id(2) == 0)
    def _():
        st_ref[...] = jnp.zeros_like(st_ref)

    row = lax.broadcasted_iota(jnp.int32, (CHUNK, C_HEAD), 0)
    tt = lax.broadcasted_iota(jnp.int32, (CHUNK, CHUNK), 0)
    ss = lax.broadcasted_iota(jnp.int32, (CHUNK, CHUNK), 1)
    txs = tt ^ ss
    lower = ss < tt

    def chunk_step(c, carry):
        r0 = pl.multiple_of(c * CHUNK, CHUNK)
        for h in range(hb):
            ls = pl.ds(h * C_HEAD, C_HEAD)
            q = q_ref[pl.ds(r0, CHUNK), ls]
            f = f_ref[pl.ds(r0, CHUNK), ls]
            v = i_ref[pl.ds(r0, CHUNK), ls].astype(BF16)
            g = g_ref[pl.ds(r0, CHUNK), ls]
            qs = q * jax.nn.sigmoid(q)
            log_sig = jnp.minimum(f, 0.0) - jnp.log1p(jnp.exp(-jnp.abs(f)))
            y = log1mlb_ref[:, ls] + log_sig
            x = loglb_ref[:, ls]
            log_f = jnp.maximum(x, y) + jnp.log1p(jnp.exp(-jnp.abs(x - y)))
            key = omlb_ref[:, ls] * jax.nn.sigmoid(-f)
            b = _cumsum_rows(log_f, row)
            qs_b = qs.astype(BF16)
            key_b = key.astype(BF16)
            scores = jnp.where(tt == ss, _dot_nt(qs_b, key_b), 0.0)
            for m in HGRN_LEVELS:
                e = jnp.exp(-jnp.abs(b - _anchor_rows(b, m, row)))
                s_l = _dot_nt((qs * e).astype(BF16), (key * e).astype(BF16))
                scores = jnp.where(lower & (txs >= m) & (txs < 2 * m), s_l, scores)
            st = st_ref[h]
            o = _dot(scores.astype(BF16), v) + _dot_nt((qs * jnp.exp(b)).astype(BF16), st.astype(BF16))
            b_last = b[CHUNK - 1:CHUNK, :]
            kdec = (key * jnp.exp(b_last - b)).astype(BF16)
            st_ref[h] = st * jnp.exp(b_last) + _dot_tn(v, kdec)
            o = o * lax.rsqrt(jnp.mean(o * o, axis=-1, keepdims=True) + RMS_EPS)
            o = o * gain_ref[:, ls] * (g * jax.nn.sigmoid(g))
            o_ref[pl.ds(r0, CHUNK), ls] = o.astype(o_ref.dtype)
        return carry

    lax.fori_loop(0, n_chunk, chunk_step, 0)


def hgrn2(cols, lb, norm_gain, bsz, seq, hb=2, tc=512):
    n, c4 = cols.shape
    c = c4 // 4
    wl = hb * C_HEAD
    nhb = c // wl
    nt = seq // tc
    lb = lb.astype(F32).reshape(1, c)
    loglb = jnp.log(lb)
    log1mlb = jnp.log1p(-lb)
    omlb = 1.0 - lb

    def col(k):
        return pl.BlockSpec((tc, wl), lambda b, h, t, k=k: (b * nt + t, k * nhb + h))

    vec = pl.BlockSpec((1, wl), lambda b, h, t: (0, h))
    return pl.pallas_call(
        functools.partial(_hgrn2_body, hb=hb, n_chunk=tc // CHUNK),
        grid=(bsz, nhb, nt),
        in_specs=[col(0), col(1), col(2), col(3), vec, vec, vec, vec],
        out_specs=pl.BlockSpec((tc, wl), lambda b, h, t: (b * nt + t, h)),
        out_shape=jax.ShapeDtypeStruct((n, c), BF16),
        scratch_shapes=[pltpu.VMEM((hb, C_HEAD, C_HEAD), F32)],
        compiler_params=_cparams(("parallel", "parallel", "arbitrary")),
        name="hgrn2",
    )(cols, cols, cols, cols, loglb, log1mlb, omlb, norm_gain.reshape(1, c))


B_WIDTH = 2048
B_HEAD = 64
B_HEADS = B_WIDTH // B_HEAD
B_LORA_PAD = 128
B_GATE_LORA = 256
B_COLS_PAD = 3 * B_WIDTH + 2 * B_LORA_PAD + B_GATE_LORA


def _head_pair_ones():
    r = lax.broadcasted_iota(jnp.int32, (LANES, LANES), 0) // B_HEAD
    c = lax.broadcasted_iota(jnp.int32, (LANES, LANES), 1) // B_HEAD
    return (r == c).astype(BF16)


def _head_sums(x, ones):
    outs = []
    for j in range(x.shape[1] // LANES):
        hi, lo = _split_bf16(x[:, j * LANES:(j + 1) * LANES])
        outs.append(_dot(hi, ones) + _dot(lo, ones))
    return outs[0] if len(outs) == 1 else jnp.concatenate(outs, axis=1)


def _dot3(a, wh, wl):
    ah, al = _split_bf16(a)
    return _dot(ah, wh) + (_dot(al, wh) + _dot(ah, wl))


def _softplus(z):
    return jnp.maximum(z, 0.0) + jnp.log1p(jnp.exp(-jnp.abs(z)))


def _rwkv_prep_body(c_ref, p_ref, mu_ref, w0_ref, a0_ref, kk_ref, ka_ref,
                    wuh_ref, wul_ref, auh_ref, aul_ref, gu_ref,
                    r_o, lw_o, k_o, v_o, kk_o, a_o, g_o, *, tiles_per_seq):
    i = pl.program_id(0)
    x = c_ref[...]
    tm = x.shape[0]
    row = lax.broadcasted_iota(jnp.int32, x.shape, 0)
    prev = jnp.where(i % tiles_per_seq == 0, 0.0, p_ref[7:8, :])
    shifted = jnp.where(row == 0, prev, pltpu.roll(x, 1, axis=0))
    mixed = x + (shifted - x) * mu_ref[...]
    w = B_WIDTH
    r = mixed[:, 0:w]
    k = mixed[:, w:2 * w]
    v = mixed[:, 2 * w:3 * w]
    wd = mixed[:, 3 * w:3 * w + B_LORA_PAD]
    ad = mixed[:, 3 * w + B_LORA_PAD:3 * w + 2 * B_LORA_PAD]
    gd = mixed[:, 3 * w + 2 * B_LORA_PAD:]
    wlog = -_softplus(-(w0_ref[...] + _dot3(jnp.tanh(wd), wuh_ref[...], wul_ref[...]))) - 0.5
    a = jax.nn.sigmoid(a0_ref[...] + _dot3(ad, auh_ref[...], aul_ref[...]))
    g = _dot(jax.nn.sigmoid(gd).astype(BF16), gu_ref[...])
    kk = k * kk_ref[...]
    ssq = _head_sums(kk * kk, _head_pair_ones())
    kk = kk / jnp.maximum(jnp.sqrt(ssq), 1e-12)
    r_o[...] = r
    lw_o[...] = -jnp.exp(wlog)
    k_o[...] = k * (1.0 + (a - 1.0) * ka_ref[...])
    v_o[...] = v
    kk_o[...] = kk
    a_o[...] = a
    g_o[...] = g


def _pad_rows(w, rows):
    return jnp.zeros((rows, w.shape[1]), w.dtype).at[:w.shape[0]].set(w)


def rwkv_prep(cols, mu_pad, w0, w_up, a0, a_up, g_up, k_k, k_a, seq, tm=256):
    n, cw = cols.shape
    w = B_WIDTH
    wuh, wul = _split_bf16(_pad_rows(w_up, B_LORA_PAD))
    auh, aul = _split_bf16(_pad_rows(a_up, B_LORA_PAD))
    row = pl.BlockSpec((tm, w), lambda i: (i, 0))
    vec = pl.BlockSpec((1, w), lambda i: (0, 0))
    lora = pl.BlockSpec((B_LORA_PAD, w), lambda i: (0, 0))
    outs = pl.pallas_call(
        functools.partial(_rwkv_prep_body, tiles_per_seq=seq // tm),
        grid=(n // tm,),
        in_specs=[pl.BlockSpec((tm, cw), lambda i: (i, 0)),
                  pl.BlockSpec((8, cw), lambda i: (jnp.maximum(i * (tm // 8) - 1, 0), 0)),
                  pl.BlockSpec((1, cw), lambda i: (0, 0)),
                  vec, vec, vec, vec, lora, lora, lora, lora,
                  pl.BlockSpec((B_GATE_LORA, w), lambda i: (0, 0))],
        out_specs=[row] * 7,
        out_shape=[jax.ShapeDtypeStruct((n, w), F32)] * 7,
        compiler_params=_cparams(("parallel",)),
        name="rwkv_prep",
    )(cols, cols, mu_pad.reshape(1, cw), w0.reshape(1, w), a0.reshape(1, w), k_k.reshape(1, w),
      k_a.reshape(1, w), wuh, wul, auh, aul, g_up.astype(BF16))
    return outs


def _rwkv_scan_body(r_ref, lw_ref, k_ref, v_ref, kk_ref, a_ref, y_ref, st_ref, *, hb, n_chunk):
    @pl.when(pl.program_id(2) == 0)
    def _():
        st_ref[...] = jnp.zeros_like(st_ref)

    L = CHUNK
    row = lax.broadcasted_iota(jnp.int32, (L, B_HEAD), 0)
    tt = lax.broadcasted_iota(jnp.int32, (L, L), 0)
    ss = lax.broadcasted_iota(jnp.int32, (L, L), 1)
    strict = ss < tt
    incl = ss <= tt

    def chunk_step(c, carry):
        r0 = pl.multiple_of(c * L, L)
        for h in range(hb):
            sl = pl.ds(r0, L)
            r = r_ref[h, sl, :]
            lw = lw_ref[h, sl, :]
            k = k_ref[h, sl, :]
            v = v_ref[h, sl, :]
            kk = kk_ref[h, sl, :]
            a = a_ref[h, sl, :]
            lp = _cumsum_rows(lw, row)
            p = jnp.exp(lp)
            inv_p = jnp.exp(-lp)
            at = -kk * jnp.exp(lp - lw)
            bt = kk * a * inv_p
            kt = k * inv_p
            ar = jnp.concatenate([at, r * p], axis=0)
            s0 = st_ref[h]
            m_b = _dot_nt(ar, bt)
            m_k = _dot_nt(ar, kt)
            m_s = _dot_nt(ar, s0)
            nmat = jnp.where(strict, m_b[:L], 0.0)
            z = m_s[:L] + _dot(jnp.where(strict, m_k[:L], 0.0), v)
            lvl = 1
            while True:
                z = z + _dot(nmat, z)
                lvl *= 2
                if lvl >= L:
                    break
                nmat = _dot(nmat, nmat)
            y = (m_s[L:] + _dot(jnp.where(incl, m_b[L:], 0.0), z)
                 + _dot(jnp.where(incl, m_k[L:], 0.0), v))
            p_last = p[L - 1:L, :]
            st_ref[h] = s0 * p_last + _dot_tn(z, bt * p_last) + _dot_tn(v, kt * p_last)
            y_ref[h, sl, :] = y
        return carry

    lax.fori_loop(0, n_chunk, chunk_step, 0)


def rwkv_scan(r, lw, k, v, kk, a, bsz, seq, hb=4, tc=512):
    nh, n, hd = r.shape
    nt = seq // tc
    blk = pl.BlockSpec((hb, tc, hd), lambda b, h, t: (h, b * nt + t, 0))
    return pl.pallas_call(
        functools.partial(_rwkv_scan_body, hb=hb, n_chunk=tc // CHUNK),
        grid=(bsz, nh // hb, nt),
        in_specs=[blk] * 6,
        out_specs=blk,
        out_shape=jax.ShapeDtypeStruct((nh, n, hd), F32),
        scratch_shapes=[pltpu.VMEM((hb, hd, hd), F32)],
        compiler_params=_cparams(("parallel", "parallel", "arbitrary")),
        name="rwkv_scan",
    )(r, lw, k, v, kk, a)


def _rwkv_post_body(y_ref, r_ref, k_ref, v_ref, g_ref, rk_ref, lg_ref, lb_ref, o_ref):
    ones = _head_pair_ones()
    y = y_ref[...]
    mean = _head_sums(y, ones) * (1.0 / B_HEAD)
    yc = y - mean
    var = _head_sums(yc * yc, ones) * (1.0 / B_HEAD)
    yn = yc * lax.rsqrt(var + GN_EPS) * lg_ref[...] + lb_ref[...]
    v = v_ref[...]
    bonus = _head_sums(r_ref[...] * k_ref[...] * rk_ref[...], ones) * v
    o_ref[...] = ((yn + bonus) * g_ref[...]).astype(o_ref.dtype)


def rwkv_post(y, r, k, v, g, r_k, ln_gain, ln_bias, tm=256):
    n, w = y.shape
    row = pl.BlockSpec((tm, w), lambda i: (i, 0))
    vec = pl.BlockSpec((1, w), lambda i: (0, 0))
    return pl.pallas_call(
        _rwkv_post_body,
        grid=(n // tm,),
        in_specs=[row] * 5 + [vec] * 3,
        out_specs=row,
        out_shape=jax.ShapeDtypeStruct((n, w), BF16),
        compiler_params=_cparams(("parallel",)),
        name="rwkv_post",
    )(y, r, k, v, g, r_k.reshape(1, w), ln_gain.reshape(1, w), ln_bias.reshape(1, w))


def _to_heads(z):
    n = z.shape[0]
    return z.reshape(n, B_HEADS, B_HEAD).transpose(1, 0, 2)


def mixer_rwkv7(cols_b, mu_pad, w0, w_up, a0, a_up, g_up, k_k, k_a, r_k, ln_gain, ln_bias, bsz, seq):
    r, lw, k, v, kk, a, g = rwkv_prep(cols_b, mu_pad, w0, w_up, a0, a_up, g_up, k_k, k_a, seq)
    yh = rwkv_scan(*[_to_heads(z) for z in (r, lw, k, v, kk, a)], bsz, seq)
    y = yh.transpose(1, 0, 2).reshape(r.shape)
    return rwkv_post(y, r, k, v, g, r_k.reshape(-1), ln_gain, ln_bias)


A_Q_RANK = 768
A_KV_RANK = 512
IDX_DIM = 64
IDX_HEADS = 16
A_HEADS = 16
A_HEAD_DIM = 128
A_COLS_PAD = A_Q_RANK + A_KV_RANK + 2 * LANES
INT_MIN = -2 ** 31


def _dsa_prep_body(c_ref, qn_ref, kn_ref, g_ref, b_ref, q_o, c_o, k_o, w_o):
    x = c_ref[...]
    ql = x[:, :A_Q_RANK]
    q_o[...] = (ql * lax.rsqrt(jnp.mean(ql * ql, axis=-1, keepdims=True) + RMS_EPS) * qn_ref[...]).astype(q_o.dtype)
    kv = x[:, A_Q_RANK:A_Q_RANK + A_KV_RANK]
    c_o[...] = (kv * lax.rsqrt(jnp.mean(kv * kv, axis=-1, keepdims=True) + RMS_EPS) * kn_ref[...]).astype(c_o.dtype)
    o = A_Q_RANK + A_KV_RANK
    ki = x[:, o:o + LANES]
    valid = lax.broadcasted_iota(jnp.int32, ki.shape, 1) < IDX_DIM
    mu = jnp.sum(jnp.where(valid, ki, 0.0), axis=-1, keepdims=True) * (1.0 / IDX_DIM)
    kc = jnp.where(valid, ki - mu, 0.0)
    var = jnp.sum(kc * kc, axis=-1, keepdims=True) * (1.0 / IDX_DIM)
    kn = kc * lax.rsqrt(var + LN_EPS) * g_ref[...] + b_ref[...]
    k_o[...] = kn[:, :IDX_DIM].astype(k_o.dtype)
    w_o[...] = x[:, o + LANES:o + 2 * LANES] * (IDX_HEADS ** -0.5 * IDX_DIM ** -0.5)


def dsa_prep(cols, q_norm, kv_norm, k_gain, k_bias, tm=512):
    n, cw = cols.shape
    pad = jnp.zeros((LANES - IDX_DIM,), F32)
    return pl.pallas_call(
        _dsa_prep_body,
        grid=(n // tm,),
        in_specs=[pl.BlockSpec((tm, cw), lambda i: (i, 0)),
                  pl.BlockSpec((1, A_Q_RANK), lambda i: (0, 0)),
                  pl.BlockSpec((1, A_KV_RANK), lambda i: (0, 0)),
                  pl.BlockSpec((1, LANES), lambda i: (0, 0)),
                  pl.BlockSpec((1, LANES), lambda i: (0, 0))],
        out_specs=[pl.BlockSpec((tm, A_Q_RANK), lambda i: (i, 0)),
                   pl.BlockSpec((tm, A_KV_RANK), lambda i: (i, 0)),
                   pl.BlockSpec((tm, IDX_DIM), lambda i: (i, 0)),
                   pl.BlockSpec((tm, LANES), lambda i: (i, 0))],
        out_shape=[jax.ShapeDtypeStruct((n, A_Q_RANK), BF16), jax.ShapeDtypeStruct((n, A_KV_RANK), BF16),
                   jax.ShapeDtypeStruct((n, IDX_DIM), BF16), jax.ShapeDtypeStruct((n, LANES), F32)],
        compiler_params=_cparams(("parallel",)),
        name="dsa_prep",
    )(cols, q_norm.reshape(1, -1), kv_norm.reshape(1, -1),
      jnp.concatenate([k_gain, pad]).reshape(1, LANES), jnp.concatenate([k_bias, pad]).reshape(1, LANES))


def _head_mm_body(a_ref, w_ref, o_ref, *, scale):
    o_ref[0] = (_dot(a_ref[...], w_ref[0]) * scale).astype(o_ref.dtype)


def head_matmul(a, w, scale, tm=1024):
    n = a.shape[0]
    nh, k, f = w.shape
    tm = min(tm, n)
    return pl.pallas_call(
        functools.partial(_head_mm_body, scale=scale),
        grid=(nh, n // tm),
        in_specs=[pl.BlockSpec((tm, k), lambda h, i: (i, h)),
                  pl.BlockSpec((1, k, f), lambda h, i: (h, 0, 0))],
        out_specs=pl.BlockSpec((1, tm, f), lambda h, i: (h, i, 0)),
        out_shape=jax.ShapeDtypeStruct((nh, n, f), BF16),
        compiler_params=_cparams(("parallel", "parallel")),
        name="head_matmul",
    )(a, w)


DSA_KT = 256


def _dsa_attend_body(qi_ref, wi_ref, qa_ref, ck_ref, kx_ref, wuv_ref, sl_ref, o_ref,
                     keys_ref, m_ref, l_ref, acc_ref, *, top_k):
    j = pl.program_id(1)
    kt_n = ((j + 1) * Q_BLOCK + DSA_KT - 1) // DSA_KT
    nq = Q_BLOCK
    nh = A_HEADS
    qrow = lax.broadcasted_iota(jnp.int32, (nq, 1), 0)
    qpos = j * Q_BLOCK + qrow
    limit = (qpos // CHUNK + 1) * CHUNK
    lane = lax.broadcasted_iota(jnp.int32, (nq, DSA_KT), 1)

    qi = qi_ref[...].reshape(nh * nq, IDX_DIM)
    wi = wi_ref[...]

    def score_tile(kt, carry):
        k0 = pl.multiple_of(kt * DSA_KT, DSA_KT)
        rel = jnp.maximum(_dot_nt(qi, kx_ref[pl.ds(k0, DSA_KT), :]), 0.0)
        score = jnp.sum(rel.reshape(nh, nq, DSA_KT) * wi, axis=0)
        bits = lax.bitcast_convert_type(score, jnp.int32)
        key = bits ^ ((bits >> 31) & 0x7FFFFFFF)
        keys_ref[:, pl.ds(k0, DSA_KT)] = jnp.where(k0 + lane < limit, key, INT_MIN)
        return carry

    lax.fori_loop(0, kt_n, score_tile, 0)

    def bit_step(i, ans):
        cand = ans | lax.shift_left(jnp.int32(1), 31 - i)
        cand_s = cand ^ INT_MIN

        def count_tile(kt, cnt):
            k0 = pl.multiple_of(kt * DSA_KT, DSA_KT)
            ge = (keys_ref[:, pl.ds(k0, DSA_KT)] >= cand_s).astype(jnp.int32)
            return cnt + ge[:, :LANES] + ge[:, LANES:]

        cnt = lax.fori_loop(0, kt_n, count_tile, jnp.zeros((nq, LANES), jnp.int32))
        total = jnp.sum(cnt, axis=-1, keepdims=True)
        return jnp.where(total >= top_k, cand, ans)

    ans = lax.fori_loop(0, 32, bit_step, jnp.zeros((nq, 1), jnp.int32))
    thr = ans ^ INT_MIN

    m_ref[...] = jnp.full_like(m_ref, -jnp.inf)
    l_ref[...] = jnp.zeros_like(l_ref)
    acc_ref[...] = jnp.zeros_like(acc_ref)
    qa = qa_ref[...].reshape(nh * nq, A_KV_RANK)
    slopes = sl_ref[...]

    def attend_tile(kt, carry):
        k0 = pl.multiple_of(kt * DSA_KT, DSA_KT)
        ck = ck_ref[pl.ds(k0, DSA_KT), :]
        key = keys_ref[:, pl.ds(k0, DSA_KT)]
        sel = (key >= thr) & (key > INT_MIN)
        dist = jnp.abs(qpos - (k0 + lane)).astype(F32)
        s = _dot_nt(qa, ck).reshape(nh, nq, DSA_KT) - slopes * dist
        s = jnp.where(sel, s, -jnp.inf).reshape(nh * nq, DSA_KT)
        m_old = m_ref[...]
        m_new = jnp.maximum(m_old, jnp.max(s, axis=-1, keepdims=True))
        m_safe = jnp.where(m_new == -jnp.inf, 0.0, m_new)
        p = jnp.exp(s - m_safe)
        alpha = jnp.exp(m_old - m_safe)
        l_ref[...] = alpha * l_ref[...] + jnp.sum(p, axis=-1, keepdims=True)
        acc_ref[...] = alpha * acc_ref[...] + _dot(p.astype(BF16), ck)
        m_ref[...] = m_new
        return carry

    lax.fori_loop(0, kt_n, attend_tile, 0)
    o_lat = (acc_ref[...] / l_ref[...]).astype(BF16).reshape(nh, nq, A_KV_RANK)
    for h in range(nh):
        o_ref[:, h * A_HEAD_DIM:(h + 1) * A_HEAD_DIM] = _dot(o_lat[h], wuv_ref[h]).astype(o_ref.dtype)


def dsa_attend(qi, wi, qa, ckv, kx, w_uv, bsz, seq):
    n = ckv.shape[0]
    nblk = seq // Q_BLOCK
    top_k = min(TOPK_MAX, seq // 4)
    start = 2.0 ** (-8.0 / A_HEADS)
    slopes = jnp.asarray([start ** (h + 1) for h in range(A_HEADS)], F32).reshape(A_HEADS, 1, 1)

    def qblk(last):
        return pl.BlockSpec((A_HEADS, Q_BLOCK, last), lambda b, j: (0, b * nblk + j, 0))

    return pl.pallas_call(
        functools.partial(_dsa_attend_body, top_k=top_k),
        grid=(bsz, nblk),
        in_specs=[qblk(IDX_DIM), qblk(1), qblk(A_KV_RANK),
                  pl.BlockSpec((seq, A_KV_RANK), lambda b, j: (b, 0)),
                  pl.BlockSpec((seq, IDX_DIM), lambda b, j: (b, 0)),
                  pl.BlockSpec((A_HEADS, A_KV_RANK, A_HEAD_DIM), lambda b, j: (0, 0, 0)),
                  pl.BlockSpec((A_HEADS, 1, 1), lambda b, j: (0, 0, 0))],
        out_specs=pl.BlockSpec((Q_BLOCK, A_HEADS * A_HEAD_DIM), lambda b, j: (b * nblk + j, 0)),
        out_shape=jax.ShapeDtypeStruct((n, A_HEADS * A_HEAD_DIM), BF16),
        scratch_shapes=[pltpu.VMEM((Q_BLOCK, seq), jnp.int32),
                        pltpu.VMEM((A_HEADS * Q_BLOCK, 1), F32),
                        pltpu.VMEM((A_HEADS * Q_BLOCK, 1), F32),
                        pltpu.VMEM((A_HEADS * Q_BLOCK, A_KV_RANK), F32)],
        compiler_params=_cparams(("parallel", "arbitrary")),
        name="dsa_attend",
    )(qi, wi, qa, ckv, kx, w_uv, slopes)


def mixer_dsa(cols_a, q_norm, kv_norm, w_uq, w_uk, w_uv, w_idx_q, k_gain, k_bias, bsz, seq):
    n = cols_a.shape[0]
    q_lat, c_kv, k_idx, w_idx = dsa_prep(cols_a, q_norm, kv_norm, k_gain, k_bias)
    q = matmul(q_lat, w_uq.reshape(A_Q_RANK, -1).astype(BF16), out_dtype=BF16)
    qa = head_matmul(q, w_uk.transpose(1, 2, 0).astype(BF16), A_HEAD_DIM ** -0.5)
    qi = matmul(q_lat, w_idx_q.reshape(A_Q_RANK, -1).astype(BF16), out_dtype=BF16)
    qi = qi.reshape(n, IDX_HEADS, IDX_DIM).transpose(1, 0, 2)
    wi = w_idx[:, :IDX_HEADS].T.reshape(IDX_HEADS, n, 1)
    return dsa_attend(qi, wi, qa, c_kv, k_idx, w_uv.transpose(1, 0, 2).astype(BF16), bsz, seq)


A_SPLITS = (A_Q_RANK, A_KV_RANK, IDX_DIM, IDX_HEADS)
A_COLS = sum(A_SPLITS)
DECAY_LORA = 96
AAA_LORA = 96


def _pad_cols(w, sizes, padded):
    out, o = [], 0
    for s, p in zip(sizes, padded):
        piece = w[..., o:o + s]
        if p > s:
            piece = jnp.concatenate([piece, jnp.zeros(w.shape[:-1] + (p - s,), w.dtype)], axis=-1)
        out.append(piece)
        o += s
    return jnp.concatenate(out, axis=-1)


def kernel(x, ln1_gain, ln1_bias, ln2_gain, ln2_bias, w_in_even, a_q_norm, a_kv_norm, a_w_uq, a_w_uk, a_w_uv, a_w_idx_q, a_idx_k_gain, a_idx_k_bias, b_mu, b_w0, b_w_up, b_a0, b_a_up, b_g_up, b_k_k, b_k_a, b_r_k, b_ln_gain, b_ln_bias, w_out_even, w_in_odd, c_lb_logits, c_norm_gain, w_out_odd, router_group, router_group_bias, router_expert, router_expert_bias, moe_w_gate, moe_w_up, moe_w_down):
    bsz, seq, d = x.shape
    n = bsz * seq
    lb_table = jnp.cumsum(jax.nn.softmax(c_lb_logits.astype(F32), axis=0), axis=0)
    lb_table = lb_table - lb_table[:1]
    a_sizes, a_padded = A_SPLITS, (A_Q_RANK, A_KV_RANK, LANES, LANES)
    b_sizes = (3 * B_WIDTH, DECAY_LORA, AAA_LORA, B_GATE_LORA)
    b_padded = (3 * B_WIDTH, B_LORA_PAD, B_LORA_PAD, B_GATE_LORA)
    xf = x.reshape(n, d)
    xb = xf.astype(BF16)
    for layer in range(DEPTH):
        j = layer // 2
        if layer % 2 == 0:
            w_in = w_in_even[j]
            w_a = _pad_cols(w_in[:, :A_COLS], a_sizes, a_padded).astype(BF16)
            w_b = _pad_cols(w_in[:, A_COLS:], b_sizes, b_padded).astype(BF16)
            cols_a = matmul(xb, w_a, tn=512)
            cols_b = matmul(xb, w_b, tn=512)
            y_a = mixer_dsa(cols_a, a_q_norm[j], a_kv_norm[j], a_w_uq[j], a_w_uk[j], a_w_uv[j],
                            a_w_idx_q[j], a_idx_k_gain[j], a_idx_k_bias[j], bsz, seq)
            y_b = mixer_rwkv7(cols_b, _pad_cols(b_mu[j], b_sizes, b_padded), b_w0[j], b_w_up[j], b_a0[j],
                              b_a_up[j], b_g_up[j], b_k_k[j], b_k_a[j], b_r_k[j], b_ln_gain[j], b_ln_bias[j],
                              bsz, seq)
            mixed = jnp.concatenate([y_a, y_b], axis=-1)
            w_out = w_out_even[j]
        else:
            cols = matmul(xb, w_in_odd[j].astype(BF16))
            mixed = hgrn2(cols, lb_table[j], c_norm_gain[j], bsz, seq)
            w_out = w_out_odd[j]
        xf, xb = proj_residual_ln(mixed, w_out.astype(BF16), xf, ln1_gain[layer], ln1_bias[layer])
        y0, y1 = hier_moe(xf, xb, router_group[layer], router_group_bias[layer], router_expert[layer],
                          router_expert_bias[layer], moe_w_gate[layer].astype(BF16),
                          moe_w_up[layer].astype(BF16), moe_w_down[layer].astype(BF16))
        xf, xb = add_ln(xf, y0, y1, ln2_gain[layer], ln2_bias[layer])
    return xf.reshape(bsz, seq, d)
```

```python
import functools

import jax
import jax.numpy as jnp
import numpy as np
from jax import lax
from jax.experimental import pallas as pl
from jax.experimental.pallas import tpu as pltpu

F32 = jnp.float32
BF16 = jnp.bfloat16

DEPTH = 4
ALPHA = (2 * DEPTH) ** 0.25
LN_EPS = 1e-5
RMS_EPS = 1e-6
GN_EPS = 64e-5
CHUNK = 64
Q_BLOCK = 128
TOPK_MAX = 256
N_GROUPS = 4
EXPERTS_PER_GROUP = 8
N_EXPERTS = N_GROUPS * EXPERTS_PER_GROUP

LANES = 128
VMEM_LIMIT = 56 * 1024 * 1024


def _cparams(sem):
    return pltpu.CompilerParams(dimension_semantics=sem, vmem_limit_bytes=VMEM_LIMIT)


def _dot(a, b):
    return jnp.dot(a, b, preferred_element_type=F32)


def _dot_nt(a, b):
    return lax.dot_general(a, b, (((1,), (1,)), ((), ())), preferred_element_type=F32)


def _dot_tn(a, b):
    return lax.dot_general(a, b, (((0,), (0,)), ((), ())), preferred_element_type=F32)


def _bdot(a, b):
    return lax.dot_general(a, b, (((2,), (1,)), ((0,), (0,))), preferred_element_type=F32)


def _bdot_nt(a, b):
    return lax.dot_general(a, b, (((2,), (2,)), ((0,), (0,))), preferred_element_type=F32)


def _split_bf16(x):
    hi = x.astype(BF16)
    lo = (x - hi.astype(F32)).astype(BF16)
    return hi, lo


def _mm_body(a_ref, b_ref, o_ref):
    o_ref[...] = _dot(a_ref[...].astype(BF16), b_ref[...]).astype(o_ref.dtype)


def matmul(a, b, out_dtype=F32, tm=512, tn=1024):
    m, k = a.shape
    _, n = b.shape
    tm = min(tm, m)
    tn = min(tn, n)
    assert m % tm == 0 and n % tn == 0
    return pl.pallas_call(
        _mm_body,
        grid=(n // tn, m // tm),
        in_specs=[pl.BlockSpec((tm, k), lambda j, i: (i, 0)),
                  pl.BlockSpec((k, tn), lambda j, i: (0, j))],
        out_specs=pl.BlockSpec((tm, tn), lambda j, i: (i, j)),
        out_shape=jax.ShapeDtypeStruct((m, n), out_dtype),
        compiler_params=_cparams(("parallel", "parallel")),
        name="matmul",
    )(a, b)


def _ln_rows(z, gain, bias):
    mu = jnp.mean(z, axis=-1, keepdims=True)
    zc = z - mu
    var = jnp.mean(zc * zc, axis=-1, keepdims=True)
    return zc * lax.rsqrt(var + LN_EPS) * gain + bias


def _proj_ln_body(a_ref, w_ref, x_ref, g_ref, b_ref, o_ref, ob_ref, *, nk):
    kk = pl.program_id(1)

    @pl.when(kk == 0)
    def _():
        o_ref[...] = jnp.zeros_like(o_ref)

    o_ref[...] += _dot(a_ref[...].astype(BF16), w_ref[...])

    @pl.when(kk == nk - 1)
    def _():
        y = _ln_rows(ALPHA * x_ref[...] + o_ref[...], g_ref[...], b_ref[...])
        o_ref[...] = y
        ob_ref[...] = y.astype(BF16)


def proj_residual_ln(a, w, x, gain, bias, tm=512, tk=512):
    m, k = a.shape
    d = w.shape[1]
    tm = min(tm, m)
    nk = k // tk
    assert nk >= 2
    return pl.pallas_call(
        functools.partial(_proj_ln_body, nk=nk),
        grid=(m // tm, nk),
        in_specs=[pl.BlockSpec((tm, tk), lambda i, kk: (i, kk)),
                  pl.BlockSpec((tk, d), lambda i, kk: (kk, 0)),
                  pl.BlockSpec((tm, d), lambda i, kk: (i, 0), pipeline_mode=pl.Buffered(1)),
                  pl.BlockSpec((1, d), lambda i, kk: (0, 0)),
                  pl.BlockSpec((1, d), lambda i, kk: (0, 0))],
        out_specs=[pl.BlockSpec((tm, d), lambda i, kk: (i, 0)),
                   pl.BlockSpec((tm, d), lambda i, kk: (i, 0), pipeline_mode=pl.Buffered(1))],
        out_shape=[jax.ShapeDtypeStruct((m, d), F32), jax.ShapeDtypeStruct((m, d), BF16)],
        compiler_params=_cparams(("parallel", "arbitrary")),
        name="proj_residual_ln",
    )(a, w, x, gain.reshape(1, d), bias.reshape(1, d))


def _add_ln_body(x_ref, y0_ref, y1_ref, g_ref, b_ref, o_ref, ob_ref):
    z = ALPHA * x_ref[...] + (y0_ref[...].astype(F32) + y1_ref[...].astype(F32))
    y = _ln_rows(z, g_ref[...], b_ref[...])
    o_ref[...] = y
    ob_ref[...] = y.astype(BF16)


def add_ln(x, y0, y1, gain, bias, tm=256):
    m, d = x.shape
    row = pl.BlockSpec((tm, d), lambda i: (i, 0))
    vec = pl.BlockSpec((1, d), lambda i: (0, 0))
    return pl.pallas_call(
        _add_ln_body,
        grid=(m // tm,),
        in_specs=[row, row, row, vec, vec],
        out_specs=[row, row],
        out_shape=[jax.ShapeDtypeStruct((m, d), F32), jax.ShapeDtypeStruct((m, d), BF16)],
        compiler_params=_cparams(("parallel",)),
        name="add_ln",
    )(x, y0, y1, gain.reshape(1, d), bias.reshape(1, d))


def _router_body(x_ref, wh_ref, wl_ref, b_ref, eid_ref, gate_ref):
    xh, xl = _split_bf16(x_ref[...])
    wh = wh_ref[...]
    logits = _dot(xh, wh) + (_dot(xl, wh) + _dot(xh, wl_ref[...])) + b_ref[...]
    lane = lax.broadcasted_iota(jnp.int32, logits.shape, 1)
    neg = jnp.float32(-jnp.inf)
    is_g = lane < N_GROUPS
    gl = jnp.where(is_g, logits, neg)
    gmax = jnp.max(gl, axis=-1, keepdims=True)
    g_idx = jnp.min(jnp.where(gl == gmax, lane, LANES), axis=-1, keepdims=True)
    g_p = 1.0 / jnp.sum(jnp.exp(gl - gmax), axis=-1, keepdims=True)
    e_lane = lane - N_GROUPS
    in_grp = (e_lane >= g_idx * EXPERTS_PER_GROUP) & (e_lane < (g_idx + 1) * EXPERTS_PER_GROUP)
    el = jnp.where(in_grp, logits, neg)
    emax = jnp.max(el, axis=-1, keepdims=True)
    pe = jnp.exp(el - emax)
    pe = pe / jnp.sum(pe, axis=-1, keepdims=True)
    v1 = jnp.max(pe, axis=-1, keepdims=True)
    i1 = jnp.min(jnp.where(in_grp & (pe == v1), lane, LANES), axis=-1, keepdims=True)
    rest = in_grp & (lane != i1)
    pe2 = jnp.where(rest, pe, -1.0)
    v2 = jnp.max(pe2, axis=-1, keepdims=True)
    i2 = jnp.min(jnp.where(rest & (pe2 == v2), lane, LANES), axis=-1, keepdims=True)
    den = v1 + v2
    eid_ref[...] = jnp.where(lane == 0, i1 - N_GROUPS, i2 - N_GROUPS)
    gate_ref[...] = jnp.where(lane == 0, g_p * (v1 / den), g_p * (v2 / den))


def router(x, wg, bg, we, be, tm=512):
    m, d = x.shape
    w = jnp.zeros((d, LANES), F32).at[:, :N_GROUPS].set(wg).at[:, N_GROUPS:N_GROUPS + N_EXPERTS].set(we)
    b = jnp.zeros((1, LANES), F32).at[0, :N_GROUPS].set(bg).at[0, N_GROUPS:N_GROUPS + N_EXPERTS].set(be)
    wh, wl = _split_bf16(w)
    row = pl.BlockSpec((tm, LANES), lambda i: (i, 0))
    wspec = pl.BlockSpec((d, LANES), lambda i: (0, 0))
    eid, gate = pl.pallas_call(
        _router_body,
        grid=(m // tm,),
        in_specs=[pl.BlockSpec((tm, d), lambda i: (i, 0)), wspec, wspec,
                  pl.BlockSpec((1, LANES), lambda i: (0, 0))],
        out_specs=[row, row],
        out_shape=[jax.ShapeDtypeStruct((m, LANES), jnp.int32), jax.ShapeDtypeStruct((m, LANES), F32)],
        compiler_params=_cparams(("parallel",)),
        name="router",
    )(x, wh, wl, b)
    return eid[:, :2], gate[:, :2]


def _expert_body(te_ref, tv_ref, x_ref, gate_ref, wg_ref, wu_ref, wd_ref, o_ref):
    i = pl.program_id(0)

    @pl.when(tv_ref[i] > 0)
    def _():
        xs = x_ref[...]
        hg = _dot(xs, wg_ref[0])
        hu = _dot(xs, wu_ref[0])
        act = (hg * jax.nn.sigmoid(hg)) * hu * gate_ref[...]
        o_ref[...] = _dot(act.astype(BF16), wd_ref[0]).astype(o_ref.dtype)

    @pl.when(tv_ref[i] == 0)
    def _():
        o_ref[...] = jnp.zeros_like(o_ref)


def expert_ffn(xs, gate_sorted, tile_expert, tile_valid, w_gate, w_up, w_down, tm):
    p, d = xs.shape
    f = w_gate.shape[-1]
    ntiles = p // tm
    grid_spec = pltpu.PrefetchScalarGridSpec(
        num_scalar_prefetch=2,
        grid=(ntiles,),
        in_specs=[pl.BlockSpec((tm, d), lambda i, te, tv: (i, 0)),
                  pl.BlockSpec((tm, 1), lambda i, te, tv: (i, 0)),
                  pl.BlockSpec((1, d, f), lambda i, te, tv: (te[i], 0, 0)),
                  pl.BlockSpec((1, d, f), lambda i, te, tv: (te[i], 0, 0)),
                  pl.BlockSpec((1, f, d), lambda i, te, tv: (te[i], 0, 0))],
        out_specs=pl.BlockSpec((tm, d), lambda i, te, tv: (i, 0)),
    )
    return pl.pallas_call(
        _expert_body,
        grid_spec=grid_spec,
        out_shape=jax.ShapeDtypeStruct((p, d), BF16),
        compiler_params=_cparams(("arbitrary",)),
        name="expert_ffn",
    )(tile_expert, tile_valid, xs, gate_sorted, w_gate, w_up, w_down)


MOE_TILE = 256


def hier_moe(x_f32, x_bf16, wg, bg, we, be, w_gate, w_up, w_down):
    n, d = x_f32.shape
    tm = MOE_TILE
    eid, gates = router(x_f32, wg, bg, we, be)
    e_flat = eid.reshape(-1)
    onehot = (e_flat[:, None] == jnp.arange(N_EXPERTS, dtype=jnp.int32)[None, :]).astype(jnp.int32)
    counts = jnp.sum(onehot, axis=0)
    rank = jnp.sum((jnp.cumsum(onehot, axis=0) - onehot) * onehot, axis=1)
    padded = ((counts + tm - 1) // tm) * tm
    seg_end = jnp.cumsum(padded)
    seg_start = seg_end - padded
    pos = seg_start[e_flat] + rank
    p = 2 * n + N_EXPERTS * tm
    tok = jnp.arange(2 * n, dtype=jnp.int32) // 2
    tok_sorted = jnp.zeros((p,), jnp.int32).at[pos].set(tok)
    gate_sorted = jnp.zeros((p,), F32).at[pos].set(gates.reshape(-1))
    tile_start = jnp.arange(p // tm, dtype=jnp.int32) * tm
    tile_valid = (tile_start < seg_end[-1]).astype(jnp.int32)
    tile_expert = jnp.sum((tile_start[:, None] >= seg_end[None, :]).astype(jnp.int32), axis=1)
    tile_expert = jnp.minimum(tile_expert, N_EXPERTS - 1)
    last_e = jnp.max(jnp.where(tile_valid > 0, tile_expert, 0))
    tile_expert = jnp.where(tile_valid > 0, tile_expert, last_e).astype(jnp.int32)
    xs = jnp.take(x_bf16, tok_sorted, axis=0)
    ys = expert_ffn(xs, gate_sorted.reshape(p, 1), tile_expert, tile_valid, w_gate, w_up, w_down, tm)
    pos2 = pos.reshape(n, 2)
    return jnp.take(ys, pos2[:, 0], axis=0), jnp.take(ys, pos2[:, 1], axis=0)


C_HEAD = 128
HGRN_LEVELS = (1, 2, 4, 8, 16, 32)


def _cumsum_rows(x, row):
    axis = x.ndim - 2
    s = 1
    while s < x.shape[axis]:
        x = x + jnp.where(row >= s, pltpu.roll(x, s, axis=axis), 0.0)
        s *= 2
    return x


def _anchor_rows(b, m):
    n = b.shape[0]
    if m >= 8:
        parts = [jnp.broadcast_to(b[base + m:base + m + 1, :], (2 * m, b.shape[1]))
                 for base in range(0, n, 2 * m)]
        return parts[0] if len(parts) == 1 else jnp.concatenate(parts, axis=0)
    b3 = b.reshape(n // 8, 8, b.shape[1])
    sub = lax.broadcasted_iota(jnp.int32, b3.shape, 1)
    out = None
    for base in range(8 - 2 * m, -1, -2 * m):
        mid = jnp.broadcast_to(b3[:, base + m:base + m + 1, :], b3.shape)
        out = mid if out is None else jnp.where(sub < base + 2 * m, mid, out)
    return out.reshape(n, b.shape[1])


def _hgrn2_body(q_ref, f_ref, i_ref, g_ref, loglb_ref, log1mlb_ref, omlb_ref, gain_ref,
                o_ref, st_ref, *, hb, n_chunk):
    @pl.when(pl.program_id(2) == 0)
    def _():
        st_ref[...] = jnp.zeros_like(st_ref)

    row = lax.broadcasted_iota(jnp.int32, (CHUNK, C_HEAD), 0)
    tt = lax.broadcasted_iota(jnp.int32, (CHUNK, CHUNK), 0)
    ss = lax.broadcasted_iota(jnp.int32, (CHUNK, CHUNK), 1)
    txs = tt ^ ss
    lvl_id = jnp.where(tt == ss, 0, -1)
    for li, m in enumerate(HGRN_LEVELS):
        lvl_id = jnp.where((ss < tt) & (txs >= m) & (txs < 2 * m), li + 1, lvl_id)

    def chunk_step(c, carry):
        r0 = pl.multiple_of(c * CHUNK, CHUNK)
        for h in range(hb):
            ls = pl.ds(h * C_HEAD, C_HEAD)
            q = q_ref[pl.ds(r0, CHUNK), ls]
            f = f_ref[pl.ds(r0, CHUNK), ls]
            v = i_ref[pl.ds(r0, CHUNK), ls].astype(BF16)
            g = g_ref[pl.ds(r0, CHUNK), ls]
            qs = q * jax.nn.sigmoid(q)
            log_sig = jnp.minimum(f, 0.0) - jnp.log1p(jnp.exp(-jnp.abs(f)))
            y = log1mlb_ref[:, ls] + log_sig
            x = loglb_ref[:, ls]
            log_f = jnp.maximum(x, y) + jnp.log1p(jnp.exp(-jnp.abs(x - y)))
            key = omlb_ref[:, ls] * jax.nn.sigmoid(-f)
            b = _cumsum_rows(log_f, row)
            qs_b = qs.astype(BF16)
            key_b = key.astype(BF16)
            scores = jnp.where(lvl_id == 0, _dot_nt(qs_b, key_b), 0.0)
            for li, m in enumerate(HGRN_LEVELS):
                e = jnp.exp(-jnp.abs(b - _anchor_rows(b, m)))
                s_l = _dot_nt((qs * e).astype(BF16), (key * e).astype(BF16))
                scores = jnp.where(lvl_id == li + 1, s_l, scores)
            st = st_ref[h]
            o = _dot(scores.astype(BF16), v) + _dot_nt((qs * jnp.exp(b)).astype(BF16), st.astype(BF16))
            b_last = b[CHUNK - 1:CHUNK, :]
            kdec = (key * jnp.exp(b_last - b)).astype(BF16)
            st_ref[h] = st * jnp.exp(b_last) + _dot_tn(v, kdec)
            o = o * lax.rsqrt(jnp.mean(o * o, axis=-1, keepdims=True) + RMS_EPS)
            o = o * gain_ref[:, ls] * (g * jax.nn.sigmoid(g))
            o_ref[pl.ds(r0, CHUNK), ls] = o.astype(o_ref.dtype)
        return carry

    lax.fori_loop(0, n_chunk, chunk_step, 0)


def hgrn2(cols, lb, norm_gain, bsz, seq, hb=4, tc=512):
    n, c4 = cols.shape
    c = c4 // 4
    wl = hb * C_HEAD
    nhb = c // wl
    nt = seq // tc
    lb = lb.astype(F32).reshape(1, c)
    loglb = jnp.log(lb)
    log1mlb = jnp.log1p(-lb)
    omlb = 1.0 - lb

    def col(k):
        return pl.BlockSpec((tc, wl), lambda b, h, t, k=k: (b * nt + t, k * nhb + h))

    vec = pl.BlockSpec((1, wl), lambda b, h, t: (0, h))
    return pl.pallas_call(
        functools.partial(_hgrn2_body, hb=hb, n_chunk=tc // CHUNK),
        grid=(bsz, nhb, nt),
        in_specs=[col(0), col(1), col(2), col(3), vec, vec, vec, vec],
        out_specs=pl.BlockSpec((tc, wl), lambda b, h, t: (b * nt + t, h)),
        out_shape=jax.ShapeDtypeStruct((n, c), BF16),
        scratch_shapes=[pltpu.VMEM((hb, C_HEAD, C_HEAD), F32)],
        compiler_params=_cparams(("parallel", "parallel", "arbitrary")),
        name="hgrn2",
    )(cols, cols, cols, cols, loglb, log1mlb, omlb, norm_gain.reshape(1, c))


B_WIDTH = 2048
B_HEAD = 64
B_HEADS = B_WIDTH // B_HEAD
B_LORA_PAD = 128
B_GATE_LORA = 256
B_COLS_PAD = 3 * B_WIDTH + 2 * B_LORA_PAD + B_GATE_LORA


def _head_pair_ones():
    r = lax.broadcasted_iota(jnp.int32, (LANES, LANES), 0) // B_HEAD
    c = lax.broadcasted_iota(jnp.int32, (LANES, LANES), 1) // B_HEAD
    return (r == c).astype(BF16)


def _head_sums(x, ones):
    outs = []
    for j in range(x.shape[1] // LANES):
        hi, lo = _split_bf16(x[:, j * LANES:(j + 1) * LANES])
        outs.append(_dot(hi, ones) + _dot(lo, ones))
    return outs[0] if len(outs) == 1 else jnp.concatenate(outs, axis=1)


def _dot3(a, wh, wl):
    ah, al = _split_bf16(a)
    return _dot(ah, wh) + (_dot(al, wh) + _dot(ah, wl))


def _softplus(z):
    return jnp.maximum(z, 0.0) + jnp.log1p(jnp.exp(-jnp.abs(z)))


def _rwkv_prep_body(c_ref, p_ref, mu_ref, w0_ref, a0_ref, kk_ref, ka_ref,
                    wuh_ref, wul_ref, auh_ref, aul_ref, gu_ref,
                    r_o, lw_o, k_o, v_o, kk_o, a_o, g_o, *, tiles_per_seq):
    i = pl.program_id(0)
    x = c_ref[...]
    tm = x.shape[0]
    row = lax.broadcasted_iota(jnp.int32, x.shape, 0)
    prev = jnp.where(i % tiles_per_seq == 0, 0.0, p_ref[7:8, :])
    shifted = jnp.where(row == 0, prev, pltpu.roll(x, 1, axis=0))
    mixed = x + (shifted - x) * mu_ref[...]
    w = B_WIDTH
    r = mixed[:, 0:w]
    k = mixed[:, w:2 * w]
    v = mixed[:, 2 * w:3 * w]
    wd = mixed[:, 3 * w:3 * w + B_LORA_PAD]
    ad = mixed[:, 3 * w + B_LORA_PAD:3 * w + 2 * B_LORA_PAD]
    gd = mixed[:, 3 * w + 2 * B_LORA_PAD:]
    wlog = -_softplus(-(w0_ref[...] + _dot3(jnp.tanh(wd), wuh_ref[...], wul_ref[...]))) - 0.5
    a = jax.nn.sigmoid(a0_ref[...] + _dot3(ad, auh_ref[...], aul_ref[...]))
    g = _dot(jax.nn.sigmoid(gd).astype(BF16), gu_ref[...])
    kk = k * kk_ref[...]
    ssq = _head_sums(kk * kk, _head_pair_ones())
    kk = kk / jnp.maximum(jnp.sqrt(ssq), 1e-12)
    r_o[...] = r
    lw_o[...] = -jnp.exp(wlog)
    k_o[...] = k * (1.0 + (a - 1.0) * ka_ref[...])
    v_o[...] = v
    kk_o[...] = kk
    a_o[...] = a
    g_o[...] = g


def _pad_rows(w, rows):
    return jnp.zeros((rows, w.shape[1]), w.dtype).at[:w.shape[0]].set(w)


def rwkv_prep(cols, mu_pad, w0, w_up, a0, a_up, g_up, k_k, k_a, seq, tm=256):
    n, cw = cols.shape
    w = B_WIDTH
    wuh, wul = _split_bf16(_pad_rows(w_up, B_LORA_PAD))
    auh, aul = _split_bf16(_pad_rows(a_up, B_LORA_PAD))
    row = pl.BlockSpec((tm, w), lambda i: (i, 0))
    vec = pl.BlockSpec((1, w), lambda i: (0, 0))
    lora = pl.BlockSpec((B_LORA_PAD, w), lambda i: (0, 0))
    outs = pl.pallas_call(
        functools.partial(_rwkv_prep_body, tiles_per_seq=seq // tm),
        grid=(n // tm,),
        in_specs=[pl.BlockSpec((tm, cw), lambda i: (i, 0)),
                  pl.BlockSpec((8, cw), lambda i: (jnp.maximum(i * (tm // 8) - 1, 0), 0)),
                  pl.BlockSpec((1, cw), lambda i: (0, 0)),
                  vec, vec, vec, vec, lora, lora, lora, lora,
                  pl.BlockSpec((B_GATE_LORA, w), lambda i: (0, 0))],
        out_specs=[row] * 7,
        out_shape=[jax.ShapeDtypeStruct((n, w), F32)] * 7,
        compiler_params=_cparams(("parallel",)),
        name="rwkv_prep",
    )(cols, cols, mu_pad.reshape(1, cw), w0.reshape(1, w), a0.reshape(1, w), k_k.reshape(1, w),
      k_a.reshape(1, w), wuh, wul, auh, aul, g_up.astype(BF16))
    return outs


def _rwkv_scan_body(r_ref, lw_ref, k_ref, v_ref, kk_ref, a_ref, y_ref, st_ref, qm_ref, ys_ref, *, hb, n_chunk):
    @pl.when(pl.program_id(2) == 0)
    def _():
        st_ref[...] = jnp.zeros_like(st_ref)

    L = CHUNK
    row = lax.broadcasted_iota(jnp.int32, (hb, L, B_HEAD), 1)
    tt = lax.broadcasted_iota(jnp.int32, (L, L), 0)
    ss = lax.broadcasted_iota(jnp.int32, (L, L), 1)
    strict = ss < tt
    incl = ss <= tt
    eye = ss == tt
    eye_f = eye.astype(F32)

    def local_step(c, carry):
        sl = pl.ds(pl.multiple_of(c * L, L), L)
        r = r_ref[:, sl, :]
        lw = lw_ref[:, sl, :]
        k = k_ref[:, sl, :]
        v = v_ref[:, sl, :]
        kk = kk_ref[:, sl, :]
        a = a_ref[:, sl, :]
        lp = _cumsum_rows(lw, row)
        p = jnp.exp(lp)
        inv_p = jnp.exp(-lp)
        at = -kk * jnp.exp(lp - lw)
        bt = kk * a * inv_p
        kt = k * inv_p
        rt = r * p
        p_last = p[:, L - 1:L, :]
        bp = bt * p_last
        kp = kt * p_last
        ar = jnp.concatenate([at, rt], axis=1)
        m_b = _bdot_nt(ar, bt)
        m_k = _bdot_nt(ar, kt)
        nmat = jnp.where(strict, m_b[:, :L], 0.0)
        a_rb = jnp.where(incl, m_b[:, L:], 0.0)
        g = _bdot(jnp.where(strict, m_k[:, :L], 0.0), v)
        tmat = eye_f + nmat
        npow = nmat
        lvl = 2
        while lvl < L:
            npow = _bdot(npow, npow)
            tmat = tmat + _bdot(npow, tmat)
            lvl *= 2
        w = _bdot(tmat, at)
        u = _bdot(tmat, g)
        qe = rt + _bdot(a_rb, w)
        yl = _bdot(a_rb, u) + _bdot(jnp.where(incl, m_k[:, L:], 0.0), v)
        qm_ref[:, c, 0:L, :] = qe
        ys_ref[:, c, 0:L, :] = yl
        for h in range(hb):
            qm_ref[h, c, L:2 * L, :] = _dot_tn(bp[h], w[h]) + jnp.where(eye, p_last[h], 0.0)
            ys_ref[h, c, L:2 * L, :] = _dot_tn(bp[h], u[h]) + _dot_tn(kp[h], v[h])
        return carry

    lax.fori_loop(0, n_chunk, local_step, 0)

    def serial_step(c, carry):
        sl = pl.ds(pl.multiple_of(c * L, L), L)
        out = _bdot(qm_ref[:, c], st_ref[...]) + ys_ref[:, c]
        y_ref[:, sl, :] = out[:, :L]
        st_ref[...] = out[:, L:]
        return carry

    lax.fori_loop(0, n_chunk, serial_step, 0)


def rwkv_scan(r, lw, k, v, kk, a, bsz, seq, hb=16, tc=256):
    nh, n, hd = r.shape
    tc = min(tc, seq)
    nt = seq // tc
    n_chunk = tc // CHUNK
    blk = pl.BlockSpec((hb, tc, hd), lambda b, h, t: (h, b * nt + t, 0))
    return pl.pallas_call(
        functools.partial(_rwkv_scan_body, hb=hb, n_chunk=n_chunk),
        grid=(bsz, nh // hb, nt),
        in_specs=[blk] * 6,
        out_specs=blk,
        out_shape=jax.ShapeDtypeStruct((nh, n, hd), F32),
        scratch_shapes=[pltpu.VMEM((hb, hd, hd), F32),
                        pltpu.VMEM((hb, n_chunk, 2 * CHUNK, hd), F32),
                        pltpu.VMEM((hb, n_chunk, 2 * CHUNK, hd), F32)],
        compiler_params=_cparams(("parallel", "parallel", "arbitrary")),
        name="rwkv_scan",
    )(r, lw, k, v, kk, a)


def _rwkv_post_body(y_ref, r_ref, k_ref, v_ref, g_ref, rk_ref, lg_ref, lb_ref, o_ref):
    ones = _head_pair_ones()
    y = y_ref[...]
    mean = _head_sums(y, ones) * (1.0 / B_HEAD)
    yc = y - mean
    var = _head_sums(yc * yc, ones) * (1.0 / B_HEAD)
    yn = yc * lax.rsqrt(var + GN_EPS) * lg_ref[...] + lb_ref[...]
    v = v_ref[...]
    bonus = _head_sums(r_ref[...] * k_ref[...] * rk_ref[...], ones) * v
    o_ref[...] = ((yn + bonus) * g_ref[...]).astype(o_ref.dtype)


def rwkv_post(y, r, k, v, g, r_k, ln_gain, ln_bias, tm=256):
    n, w = y.shape
    row = pl.BlockSpec((tm, w), lambda i: (i, 0))
    vec = pl.BlockSpec((1, w), lambda i: (0, 0))
    return pl.pallas_call(
        _rwkv_post_body,
        grid=(n // tm,),
        in_specs=[row] * 5 + [vec] * 3,
        out_specs=row,
        out_shape=jax.ShapeDtypeStruct((n, w), BF16),
        compiler_params=_cparams(("parallel",)),
        name="rwkv_post",
    )(y, r, k, v, g, r_k.reshape(1, w), ln_gain.reshape(1, w), ln_bias.reshape(1, w))


def _to_heads(z):
    n = z.shape[0]
    return z.reshape(n, B_HEADS, B_HEAD).transpose(1, 0, 2)


def mixer_rwkv7(cols_b, mu_pad, w0, w_up, a0, a_up, g_up, k_k, k_a, r_k, ln_gain, ln_bias, bsz, seq):
    r, lw, k, v, kk, a, g = rwkv_prep(cols_b, mu_pad, w0, w_up, a0, a_up, g_up, k_k, k_a, seq)
    yh = rwkv_scan(*[_to_heads(z) for z in (r, lw, k, v, kk, a)], bsz, seq)
    y = yh.transpose(1, 0, 2).reshape(r.shape)
    return rwkv_post(y, r, k, v, g, r_k.reshape(-1), ln_gain, ln_bias)


A_Q_RANK = 768
A_KV_RANK = 512
IDX_DIM = 64
IDX_HEADS = 16
A_HEADS = 16
A_HEAD_DIM = 128
A_COLS_PAD = A_Q_RANK + A_KV_RANK + 2 * LANES
INT_MIN = -2 ** 31


def _dsa_prep_body(c_ref, qn_ref, kn_ref, g_ref, b_ref, q_o, c_o, k_o, w_o):
    x = c_ref[...]
    ql = x[:, :A_Q_RANK]
    q_o[...] = (ql * lax.rsqrt(jnp.mean(ql * ql, axis=-1, keepdims=True) + RMS_EPS) * qn_ref[...]).astype(q_o.dtype)
    kv = x[:, A_Q_RANK:A_Q_RANK + A_KV_RANK]
    c_o[...] = (kv * lax.rsqrt(jnp.mean(kv * kv, axis=-1, keepdims=True) + RMS_EPS) * kn_ref[...]).astype(c_o.dtype)
    o = A_Q_RANK + A_KV_RANK
    ki = x[:, o:o + LANES]
    valid = lax.broadcasted_iota(jnp.int32, ki.shape, 1) < IDX_DIM
    mu = jnp.sum(jnp.where(valid, ki, 0.0), axis=-1, keepdims=True) * (1.0 / IDX_DIM)
    kc = jnp.where(valid, ki - mu, 0.0)
    var = jnp.sum(kc * kc, axis=-1, keepdims=True) * (1.0 / IDX_DIM)
    kn = kc * lax.rsqrt(var + LN_EPS) * g_ref[...] + b_ref[...]
    k_o[...] = kn[:, :IDX_DIM].astype(k_o.dtype)
    w_o[...] = x[:, o + LANES:o + 2 * LANES] * (IDX_HEADS ** -0.5 * IDX_DIM ** -0.5)


def dsa_prep(cols, q_norm, kv_norm, k_gain, k_bias, tm=512):
    n, cw = cols.shape
    pad = jnp.zeros((LANES - IDX_DIM,), F32)
    return pl.pallas_call(
        _dsa_prep_body,
        grid=(n // tm,),
        in_specs=[pl.BlockSpec((tm, cw), lambda i: (i, 0)),
                  pl.BlockSpec((1, A_Q_RANK), lambda i: (0, 0)),
                  pl.BlockSpec((1, A_KV_RANK), lambda i: (0, 0)),
                  pl.BlockSpec((1, LANES), lambda i: (0, 0)),
                  pl.BlockSpec((1, LANES), lambda i: (0, 0))],
        out_specs=[pl.BlockSpec((tm, A_Q_RANK), lambda i: (i, 0)),
                   pl.BlockSpec((tm, A_KV_RANK), lambda i: (i, 0)),
                   pl.BlockSpec((tm, IDX_DIM), lambda i: (i, 0)),
                   pl.BlockSpec((tm, LANES), lambda i: (i, 0))],
        out_shape=[jax.ShapeDtypeStruct((n, A_Q_RANK), BF16), jax.ShapeDtypeStruct((n, A_KV_RANK), BF16),
                   jax.ShapeDtypeStruct((n, IDX_DIM), BF16), jax.ShapeDtypeStruct((n, LANES), F32)],
        compiler_params=_cparams(("parallel",)),
        name="dsa_prep",
    )(cols, q_norm.reshape(1, -1), kv_norm.reshape(1, -1),
      jnp.concatenate([k_gain, pad]).reshape(1, LANES), jnp.concatenate([k_bias, pad]).reshape(1, LANES))


def _head_mm_body(a_ref, w_ref, o_ref, *, scale):
    o_ref[0] = (_dot(a_ref[...], w_ref[0]) * scale).astype(o_ref.dtype)


def head_matmul(a, w, scale, tm=1024):
    n = a.shape[0]
    nh, k, f = w.shape
    tm = min(tm, n)
    return pl.pallas_call(
        functools.partial(_head_mm_body, scale=scale),
        grid=(nh, n // tm),
        in_specs=[pl.BlockSpec((tm, k), lambda h, i: (i, h)),
                  pl.BlockSpec((1, k, f), lambda h, i: (h, 0, 0))],
        out_specs=pl.BlockSpec((1, tm, f), lambda h, i: (h, i, 0)),
        out_shape=jax.ShapeDtypeStruct((nh, n, f), BF16),
        compiler_params=_cparams(("parallel", "parallel")),
        name="head_matmul",
    )(a, w)


DSA_KT = 256
DSA_AT = 512
DSA_HEAD_GROUP = 4


def _dsa_attend_body(qi_ref, wi_ref, qa_ref, ck_ref, kx_ref, wuv_ref, sl_ref, o_ref,
                     keys_ref, m_ref, l_ref, acc_ref, *, top_k):
    j = pl.program_id(1)
    at_n = ((j + 1) * Q_BLOCK + DSA_AT - 1) // DSA_AT
    kt_n = at_n * (DSA_AT // DSA_KT)
    nq = Q_BLOCK
    nh = A_HEADS
    qrow = lax.broadcasted_iota(jnp.int32, (nq, 1), 0)
    qpos = j * Q_BLOCK + qrow
    limit = (qpos // CHUNK + 1) * CHUNK
    lane = lax.broadcasted_iota(jnp.int32, (nq, DSA_KT), 1)
    lane_a = lax.broadcasted_iota(jnp.int32, (nq, DSA_AT), 1)

    qi = qi_ref[...].reshape(nh * nq, IDX_DIM)
    wi = wi_ref[...]

    def score_tile(kt, carry):
        k0 = pl.multiple_of(kt * DSA_KT, DSA_KT)
        rel = jnp.maximum(_dot_nt(qi, kx_ref[pl.ds(k0, DSA_KT), :]), 0.0)
        score = jnp.sum(rel.reshape(nh, nq, DSA_KT) * wi, axis=0)
        bits = lax.bitcast_convert_type(score, jnp.int32)
        key = bits ^ ((bits >> 31) & 0x7FFFFFFF)
        keys_ref[:, pl.ds(k0, DSA_KT)] = jnp.where(k0 + lane < limit, key, INT_MIN)
        return carry

    lax.fori_loop(0, kt_n, score_tile, 0)

    def bit_step(i, ans):
        cand = ans | lax.shift_left(jnp.int32(1), 31 - i)
        cand_s = cand ^ INT_MIN

        def count_tile(kt, cnt):
            k0 = pl.multiple_of(kt * DSA_AT, DSA_AT)
            ge = (keys_ref[:, pl.ds(k0, DSA_AT)] >= cand_s).astype(jnp.int32)
            for t in range(DSA_AT // LANES):
                cnt = cnt + ge[:, t * LANES:(t + 1) * LANES]
            return cnt

        cnt = lax.fori_loop(0, at_n, count_tile, jnp.zeros((nq, LANES), jnp.int32))
        total = jnp.sum(cnt, axis=-1, keepdims=True)
        return jnp.where(total >= top_k, cand, ans)

    ans = lax.fori_loop(0, 32, bit_step, jnp.zeros((nq, 1), jnp.int32))
    thr = ans ^ INT_MIN

    m_ref[...] = jnp.full_like(m_ref, -jnp.inf)
    l_ref[...] = jnp.zeros_like(l_ref)
    acc_ref[...] = jnp.zeros_like(acc_ref)
    slopes = sl_ref[...]
    gh = DSA_HEAD_GROUP

    def attend_tile(kt, carry):
        k0 = pl.multiple_of(kt * DSA_AT, DSA_AT)
        ck = ck_ref[pl.ds(k0, DSA_AT), :]
        key = keys_ref[:, pl.ds(k0, DSA_AT)]
        mask_bias = jnp.where((key >= thr) & (key > INT_MIN), 0.0, -jnp.inf)
        neg_dist = -jnp.abs(qpos - (k0 + lane_a)).astype(F32)
        def qk(g):
            return _dot_nt(qa_ref[g * gh:(g + 1) * gh].reshape(gh * nq, A_KV_RANK), ck)

        s_next = qk(0)
        for g in range(nh // gh):
            rows = pl.ds(g * gh * nq, gh * nq)
            s = s_next.reshape(gh, nq, DSA_AT)
            if g + 1 < nh // gh:
                s_next = qk(g + 1)
            s = (s + (slopes[g * gh:(g + 1) * gh] * neg_dist + mask_bias)).reshape(gh * nq, DSA_AT)
            m_old = m_ref[rows, :]
            m_new = jnp.maximum(m_old, jnp.max(s, axis=-1, keepdims=True))
            m_safe = jnp.where(m_new == -jnp.inf, 0.0, m_new)
            p = jnp.exp(s - m_safe)
            alpha = jnp.exp(m_old - m_safe)
            l_ref[rows, :] = alpha * l_ref[rows, :] + jnp.sum(p, axis=-1, keepdims=True)
            acc_ref[rows, :] = alpha * acc_ref[rows, :] + _dot(p.astype(BF16), ck)
            m_ref[rows, :] = m_new
        return carry

    lax.fori_loop(0, at_n, attend_tile, 0)
    o_lat = (acc_ref[...] / l_ref[...]).astype(BF16).reshape(nh, nq, A_KV_RANK)
    for h in range(nh):
        o_ref[:, h * A_HEAD_DIM:(h + 1) * A_HEAD_DIM] = _dot(o_lat[h], wuv_ref[h]).astype(o_ref.dtype)


def dsa_attend(qi, wi, qa, ckv, kx, w_uv, bsz, seq):
    n = ckv.shape[0]
    nblk = seq // Q_BLOCK
    top_k = min(TOPK_MAX, seq // 4)
    start = 2.0 ** (-8.0 / A_HEADS)
    slopes = jnp.asarray([start ** (h + 1) for h in range(A_HEADS)], F32).reshape(A_HEADS, 1, 1)

    def qblk(last):
        return pl.BlockSpec((A_HEADS, Q_BLOCK, last), lambda b, j: (0, b * nblk + j, 0))

    return pl.pallas_call(
        functools.partial(_dsa_attend_body, top_k=top_k),
        grid=(bsz, nblk),
        in_specs=[qblk(IDX_DIM), qblk(1), qblk(A_KV_RANK),
                  pl.BlockSpec((seq, A_KV_RANK), lambda b, j: (b, 0)),
                  pl.BlockSpec((seq, IDX_DIM), lambda b, j: (b, 0)),
                  pl.BlockSpec((A_HEADS, A_KV_RANK, A_HEAD_DIM), lambda b, j: (0, 0, 0)),
                  pl.BlockSpec((A_HEADS, 1, 1), lambda b, j: (0, 0, 0))],
        out_specs=pl.BlockSpec((Q_BLOCK, A_HEADS * A_HEAD_DIM), lambda b, j: (b * nblk + j, 0)),
        out_shape=jax.ShapeDtypeStruct((n, A_HEADS * A_HEAD_DIM), BF16),
        scratch_shapes=[pltpu.VMEM((Q_BLOCK, seq), jnp.int32),
                        pltpu.VMEM((A_HEADS * Q_BLOCK, 1), F32),
                        pltpu.VMEM((A_HEADS * Q_BLOCK, 1), F32),
                        pltpu.VMEM((A_HEADS * Q_BLOCK, A_KV_RANK), F32)],
        compiler_params=_cparams(("parallel", "arbitrary")),
        name="dsa_attend",
    )(qi, wi, qa, ckv, kx, w_uv, slopes)


def mixer_dsa(cols_a, q_norm, kv_norm, w_uq, w_uk, w_uv, w_idx_q, k_gain, k_bias, bsz, seq):
    n = cols_a.shape[0]
    q_lat, c_kv, k_idx, w_idx = dsa_prep(cols_a, q_norm, kv_norm, k_gain, k_bias)
    q = matmul(q_lat, w_uq.reshape(A_Q_RANK, -1).astype(BF16), out_dtype=BF16)
    qa = head_matmul(q, w_uk.transpose(1, 2, 0).astype(BF16), A_HEAD_DIM ** -0.5)
    qi = matmul(q_lat, w_idx_q.reshape(A_Q_RANK, -1).astype(BF16), out_dtype=BF16)
    qi = qi.reshape(n, IDX_HEADS, IDX_DIM).transpose(1, 0, 2)
    wi = w_idx[:, :IDX_HEADS].T.reshape(IDX_HEADS, n, 1)
    return dsa_attend(qi, wi, qa, c_kv, k_idx, w_uv.transpose(1, 0, 2).astype(BF16), bsz, seq)


A_SPLITS = (A_Q_RANK, A_KV_RANK, IDX_DIM, IDX_HEADS)
A_COLS = sum(A_SPLITS)
DECAY_LORA = 96
AAA_LORA = 96


def _pad_cols(w, sizes, padded):
    out, o = [], 0
    for s, p in zip(sizes, padded):
        piece = w[..., o:o + s]
        if p > s:
            piece = jnp.concatenate([piece, jnp.zeros(w.shape[:-1] + (p - s,), w.dtype)], axis=-1)
        out.append(piece)
        o += s
    return jnp.concatenate(out, axis=-1)


def kernel(x, ln1_gain, ln1_bias, ln2_gain, ln2_bias, w_in_even, a_q_norm, a_kv_norm, a_w_uq, a_w_uk, a_w_uv, a_w_idx_q, a_idx_k_gain, a_idx_k_bias, b_mu, b_w0, b_w_up, b_a0, b_a_up, b_g_up, b_k_k, b_k_a, b_r_k, b_ln_gain, b_ln_bias, w_out_even, w_in_odd, c_lb_logits, c_norm_gain, w_out_odd, router_group, router_group_bias, router_expert, router_expert_bias, moe_w_gate, moe_w_up, moe_w_down):
    bsz, seq, d = x.shape
    n = bsz * seq
    lb_table = jnp.cumsum(jax.nn.softmax(c_lb_logits.astype(F32), axis=0), axis=0)
    lb_table = lb_table - lb_table[:1]
    a_sizes, a_padded = A_SPLITS, (A_Q_RANK, A_KV_RANK, LANES, LANES)
    b_sizes = (3 * B_WIDTH, DECAY_LORA, AAA_LORA, B_GATE_LORA)
    b_padded = (3 * B_WIDTH, B_LORA_PAD, B_LORA_PAD, B_GATE_LORA)
    xf = x.reshape(n, d)
    xb = xf.astype(BF16)
    for layer in range(DEPTH):
        j = layer // 2
        if layer % 2 == 0:
            w_in = w_in_even[j]
            w_a = _pad_cols(w_in[:, :A_COLS], a_sizes, a_padded).astype(BF16)
            w_b = _pad_cols(w_in[:, A_COLS:], b_sizes, b_padded).astype(BF16)
            cols_a = matmul(xb, w_a, tn=512)
            cols_b = matmul(xb, w_b, tn=512)
            y_a = mixer_dsa(cols_a, a_q_norm[j], a_kv_norm[j], a_w_uq[j], a_w_uk[j], a_w_uv[j],
                            a_w_idx_q[j], a_idx_k_gain[j], a_idx_k_bias[j], bsz, seq)
            y_b = mixer_rwkv7(cols_b, _pad_cols(b_mu[j], b_sizes, b_padded), b_w0[j], b_w_up[j], b_a0[j],
                              b_a_up[j], b_g_up[j], b_k_k[j], b_k_a[j], b_r_k[j], b_ln_gain[j], b_ln_bias[j],
                              bsz, seq)
            mixed = jnp.concatenate([y_a, y_b], axis=-1)
            w_out = w_out_even[j]
        else:
            cols = matmul(xb, w_in_odd[j].astype(BF16))
            mixed = hgrn2(cols, lb_table[j], c_norm_gain[j], bsz, seq)
            w_out = w_out_odd[j]
        xf, xb = proj_residual_ln(mixed, w_out.astype(BF16), xf, ln1_gain[layer], ln1_bias[layer])
        y0, y1 = hier_moe(xf, xb, router_group[layer], router_group_bias[layer], router_expert[layer],
                          router_expert_bias[layer], moe_w_gate[layer].astype(BF16),
                          moe_w_up[layer].astype(BF16), moe_w_down[layer].astype(BF16))
        xf, xb = add_ln(xf, y0, y1, ln2_gain[layer], ln2_bias[layer])
    return xf.reshape(bsz, seq, d)
```

```python
import functools

import jax
import jax.numpy as jnp
import numpy as np
from jax import lax
from jax.experimental import pallas as pl
from jax.experimental.pallas import tpu as pltpu

F32 = jnp.float32
BF16 = jnp.bfloat16

DEPTH = 4
ALPHA = (2 * DEPTH) ** 0.25
LN_EPS = 1e-5
RMS_EPS = 1e-6
GN_EPS = 64e-5
CHUNK = 64
Q_BLOCK = 128
TOPK_MAX = 256
N_GROUPS = 4
EXPERTS_PER_GROUP = 8
N_EXPERTS = N_GROUPS * EXPERTS_PER_GROUP

LANES = 128
VMEM_LIMIT = 56 * 1024 * 1024


def _cparams(sem):
    return pltpu.CompilerParams(dimension_semantics=sem, vmem_limit_bytes=VMEM_LIMIT)


def _dot(a, b):
    return jnp.dot(a, b, preferred_element_type=F32)


def _dot_nt(a, b):
    return lax.dot_general(a, b, (((1,), (1,)), ((), ())), preferred_element_type=F32)


def _dot_tn(a, b):
    return lax.dot_general(a, b, (((0,), (0,)), ((), ())), preferred_element_type=F32)


def _bdot(a, b):
    return lax.dot_general(a, b, (((2,), (1,)), ((0,), (0,))), preferred_element_type=F32)


def _bdot_nt(a, b):
    return lax.dot_general(a, b, (((2,), (2,)), ((0,), (0,))), preferred_element_type=F32)


def _split_bf16(x):
    hi = x.astype(BF16)
    lo = (x - hi.astype(F32)).astype(BF16)
    return hi, lo


def _mm_body(a_ref, b_ref, o_ref):
    o_ref[...] = _dot(a_ref[...].astype(BF16), b_ref[...]).astype(o_ref.dtype)


def matmul(a, b, out_dtype=F32, tm=512, tn=1024):
    m, k = a.shape
    _, n = b.shape
    tm = min(tm, m)
    tn = min(tn, n)
    assert m % tm == 0 and n % tn == 0
    return pl.pallas_call(
        _mm_body,
        grid=(n // tn, m // tm),
        in_specs=[pl.BlockSpec((tm, k), lambda j, i: (i, 0)),
                  pl.BlockSpec((k, tn), lambda j, i: (0, j))],
        out_specs=pl.BlockSpec((tm, tn), lambda j, i: (i, j)),
        out_shape=jax.ShapeDtypeStruct((m, n), out_dtype),
        compiler_params=_cparams(("parallel", "parallel")),
        name="matmul",
    )(a, b)


def _ln_rows(z, gain, bias):
    mu = jnp.mean(z, axis=-1, keepdims=True)
    zc = z - mu
    var = jnp.mean(zc * zc, axis=-1, keepdims=True)
    return zc * lax.rsqrt(var + LN_EPS) * gain + bias


def _proj_ln_body(a_ref, w_ref, x_ref, g_ref, b_ref, o_ref, ob_ref, *, nk):
    kk = pl.program_id(1)

    @pl.when(kk == 0)
    def _():
        o_ref[...] = jnp.zeros_like(o_ref)

    o_ref[...] += _dot(a_ref[...].astype(BF16), w_ref[...])

    @pl.when(kk == nk - 1)
    def _():
        y = _ln_rows(ALPHA * x_ref[...] + o_ref[...], g_ref[...], b_ref[...])
        o_ref[...] = y
        ob_ref[...] = y.astype(BF16)


def proj_residual_ln(a, w, x, gain, bias, tm=512, tk=512):
    m, k = a.shape
    d = w.shape[1]
    tm = min(tm, m)
    nk = k // tk
    assert nk >= 2
    return pl.pallas_call(
        functools.partial(_proj_ln_body, nk=nk),
        grid=(m // tm, nk),
        in_specs=[pl.BlockSpec((tm, tk), lambda i, kk: (i, kk)),
                  pl.BlockSpec((tk, d), lambda i, kk: (kk, 0)),
                  pl.BlockSpec((tm, d), lambda i, kk: (i, 0), pipeline_mode=pl.Buffered(1)),
                  pl.BlockSpec((1, d), lambda i, kk: (0, 0)),
                  pl.BlockSpec((1, d), lambda i, kk: (0, 0))],
        out_specs=[pl.BlockSpec((tm, d), lambda i, kk: (i, 0)),
                   pl.BlockSpec((tm, d), lambda i, kk: (i, 0), pipeline_mode=pl.Buffered(1))],
        out_shape=[jax.ShapeDtypeStruct((m, d), F32), jax.ShapeDtypeStruct((m, d), BF16)],
        compiler_params=_cparams(("parallel", "arbitrary")),
        name="proj_residual_ln",
    )(a, w, x, gain.reshape(1, d), bias.reshape(1, d))


def _add_ln_body(x_ref, y0_ref, y1_ref, g_ref, b_ref, o_ref, ob_ref):
    z = ALPHA * x_ref[...] + (y0_ref[...].astype(F32) + y1_ref[...].astype(F32))
    y = _ln_rows(z, g_ref[...], b_ref[...])
    o_ref[...] = y
    ob_ref[...] = y.astype(BF16)


def add_ln(x, y0, y1, gain, bias, tm=256):
    m, d = x.shape
    row = pl.BlockSpec((tm, d), lambda i: (i, 0))
    vec = pl.BlockSpec((1, d), lambda i: (0, 0))
    return pl.pallas_call(
        _add_ln_body,
        grid=(m // tm,),
        in_specs=[row, row, row, vec, vec],
        out_specs=[row, row],
        out_shape=[jax.ShapeDtypeStruct((m, d), F32), jax.ShapeDtypeStruct((m, d), BF16)],
        compiler_params=_cparams(("parallel",)),
        name="add_ln",
    )(x, y0, y1, gain.reshape(1, d), bias.reshape(1, d))


def _router_body(x_ref, wh_ref, wl_ref, b_ref, tri_ref, eid_ref, gate_ref, rank_ref, cnt_ref, base_ref):
    xh, xl = _split_bf16(x_ref[...])
    wh = wh_ref[...]
    logits = _dot(xh, wh) + (_dot(xl, wh) + _dot(xh, wl_ref[...])) + b_ref[...]
    lane = lax.broadcasted_iota(jnp.int32, logits.shape, 1)
    neg = jnp.float32(-jnp.inf)
    is_g = lane < N_GROUPS
    gl = jnp.where(is_g, logits, neg)
    gmax = jnp.max(gl, axis=-1, keepdims=True)
    g_idx = jnp.min(jnp.where(gl == gmax, lane, LANES), axis=-1, keepdims=True)
    g_p = 1.0 / jnp.sum(jnp.exp(gl - gmax), axis=-1, keepdims=True)
    e_lane = lane - N_GROUPS
    in_grp = (e_lane >= g_idx * EXPERTS_PER_GROUP) & (e_lane < (g_idx + 1) * EXPERTS_PER_GROUP)
    el = jnp.where(in_grp, logits, neg)
    emax = jnp.max(el, axis=-1, keepdims=True)
    pe = jnp.exp(el - emax)
    pe = pe / jnp.sum(pe, axis=-1, keepdims=True)
    v1 = jnp.max(pe, axis=-1, keepdims=True)
    i1 = jnp.min(jnp.where(in_grp & (pe == v1), lane, LANES), axis=-1, keepdims=True)
    rest = in_grp & (lane != i1)
    pe2 = jnp.where(rest, pe, -1.0)
    v2 = jnp.max(pe2, axis=-1, keepdims=True)
    i2 = jnp.min(jnp.where(rest & (pe2 == v2), lane, LANES), axis=-1, keepdims=True)
    den = v1 + v2
    eid_ref[...] = jnp.where(lane == 0, i1 - N_GROUPS, i2 - N_GROUPS)
    gate_ref[...] = jnp.where(lane == 0, g_p * (v1 / den), g_p * (v2 / den))

    @pl.when(pl.program_id(0) == 0)
    def _():
        base_ref[...] = jnp.zeros_like(base_ref)

    hit1 = lane == i1
    hit2 = lane == i2
    oh1 = hit1.astype(BF16)
    oh2 = hit2.astype(BF16)
    tri = tri_ref[...]
    cnt1 = jnp.sum(oh1.astype(F32), axis=0, keepdims=True)
    cnt2 = jnp.sum(oh2.astype(F32), axis=0, keepdims=True)
    base = base_ref[...]
    r1 = jnp.sum(jnp.where(hit1, base + _dot(tri, oh1), 0.0), axis=-1, keepdims=True)
    r2 = jnp.sum(jnp.where(hit2, base + cnt1 + _dot(tri, oh2), 0.0), axis=-1, keepdims=True)
    rank_ref[...] = jnp.where(lane == 0, r1, r2).astype(jnp.int32)
    base_ref[...] = base + cnt1 + cnt2
    cnt_ref[...] = (base + cnt1 + cnt2).astype(jnp.int32)


def router(x, wg, bg, we, be, tm=512):
    m, d = x.shape
    tm = min(tm, m)
    w = jnp.zeros((d, LANES), F32).at[:, :N_GROUPS].set(wg).at[:, N_GROUPS:N_GROUPS + N_EXPERTS].set(we)
    b = jnp.zeros((1, LANES), F32).at[0, :N_GROUPS].set(bg).at[0, N_GROUPS:N_GROUPS + N_EXPERTS].set(be)
    wh, wl = _split_bf16(w)
    tri = jnp.tril(jnp.ones((tm, tm), BF16), k=-1)
    row = pl.BlockSpec((tm, LANES), lambda i: (i, 0))
    wspec = pl.BlockSpec((d, LANES), lambda i: (0, 0))
    vec = pl.BlockSpec((1, LANES), lambda i: (0, 0))
    eid, gate, rank, cnt = pl.pallas_call(
        _router_body,
        grid=(m // tm,),
        in_specs=[pl.BlockSpec((tm, d), lambda i: (i, 0)), wspec, wspec, vec,
                  pl.BlockSpec((tm, tm), lambda i: (0, 0))],
        out_specs=[row, row, row, vec],
        out_shape=[jax.ShapeDtypeStruct((m, LANES), jnp.int32), jax.ShapeDtypeStruct((m, LANES), F32),
                   jax.ShapeDtypeStruct((m, LANES), jnp.int32), jax.ShapeDtypeStruct((1, LANES), jnp.int32)],
        scratch_shapes=[pltpu.VMEM((1, LANES), F32)],
        compiler_params=_cparams(("arbitrary",)),
        name="router",
    )(x, wh, wl, b, tri)
    return eid[:, :2], gate[:, :2], rank[:, :2], cnt[0, N_GROUPS:N_GROUPS + N_EXPERTS]


def _expert_body(te_ref, tv_ref, x_ref, gate_ref, wg_ref, wu_ref, wd_ref, o_ref, wgb_ref, wub_ref, wdb_ref):
    i = pl.program_id(0)

    @pl.when((i == 0) | (te_ref[i] != te_ref[jnp.maximum(i - 1, 0)]))
    def _():
        wgb_ref[...] = wg_ref[0, 0].astype(BF16)
        wub_ref[...] = wu_ref[0, 0].astype(BF16)
        wdb_ref[...] = wd_ref[0, 0].astype(BF16)

    @pl.when(tv_ref[i] > 0)
    def _():
        xs = x_ref[...]
        hg = _dot(xs, wgb_ref[...])
        hu = _dot(xs, wub_ref[...])
        act = (hg * jax.nn.sigmoid(hg)) * hu * gate_ref[...]
        o_ref[...] = _dot(act.astype(BF16), wdb_ref[...]).astype(o_ref.dtype)

    @pl.when(tv_ref[i] == 0)
    def _():
        o_ref[...] = jnp.zeros_like(o_ref)


def expert_ffn(xs, gate_sorted, tile_expert, tile_valid, w_gate, w_up, w_down, layer, tm):
    p, d = xs.shape
    f = w_gate.shape[-1]
    ntiles = p // tm
    grid_spec = pltpu.PrefetchScalarGridSpec(
        num_scalar_prefetch=2,
        grid=(ntiles,),
        in_specs=[pl.BlockSpec((tm, d), lambda i, te, tv: (i, 0)),
                  pl.BlockSpec((tm, 1), lambda i, te, tv: (i, 0)),
                  pl.BlockSpec((1, 1, d, f), lambda i, te, tv: (layer, te[i], 0, 0)),
                  pl.BlockSpec((1, 1, d, f), lambda i, te, tv: (layer, te[i], 0, 0)),
                  pl.BlockSpec((1, 1, f, d), lambda i, te, tv: (layer, te[i], 0, 0))],
        out_specs=pl.BlockSpec((tm, d), lambda i, te, tv: (i, 0)),
        scratch_shapes=[pltpu.VMEM((d, f), BF16), pltpu.VMEM((d, f), BF16), pltpu.VMEM((f, d), BF16)],
    )
    return pl.pallas_call(
        _expert_body,
        grid_spec=grid_spec,
        out_shape=jax.ShapeDtypeStruct((p, d), BF16),
        compiler_params=_cparams(("arbitrary",)),
        name="expert_ffn",
    )(tile_expert, tile_valid, xs, gate_sorted, w_gate, w_up, w_down)


MOE_TILE = 256


def hier_moe(x_f32, x_bf16, wg, bg, we, be, w_gate, w_up, w_down, layer):
    n, d = x_f32.shape
    tm = MOE_TILE
    eid, gates, rank, counts = router(x_f32, wg, bg, we, be)
    padded = ((counts + tm - 1) // tm) * tm
    seg_end = jnp.cumsum(padded)
    seg_start = seg_end - padded
    experts = jnp.arange(N_EXPERTS, dtype=jnp.int32)
    pos = (jnp.sum(jnp.where(eid[..., None] == experts, seg_start, 0), axis=-1) + rank).reshape(-1)
    p = 2 * n + N_EXPERTS * tm
    tok = jnp.arange(2 * n, dtype=jnp.int32) // 2
    packed = jnp.stack([tok, lax.bitcast_convert_type(gates.reshape(-1), jnp.int32)], axis=-1)
    packed = jnp.zeros((p, 2), jnp.int32).at[pos].set(packed)
    tok_sorted = packed[:, 0]
    gate_sorted = lax.bitcast_convert_type(packed[:, 1], F32)
    tile_start = jnp.arange(p // tm, dtype=jnp.int32) * tm
    tile_valid = (tile_start < seg_end[-1]).astype(jnp.int32)
    tile_expert = jnp.sum((tile_start[:, None] >= seg_end[None, :]).astype(jnp.int32), axis=1)
    tile_expert = jnp.minimum(tile_expert, N_EXPERTS - 1)
    last_e = jnp.max(jnp.where(tile_valid > 0, tile_expert, 0))
    tile_expert = jnp.where(tile_valid > 0, tile_expert, last_e).astype(jnp.int32)
    xs = jnp.take(x_bf16, tok_sorted, axis=0)
    ys = expert_ffn(xs, gate_sorted.reshape(p, 1), tile_expert, tile_valid, w_gate, w_up, w_down, layer, tm)
    pos2 = pos.reshape(n, 2)
    return jnp.take(ys, pos2[:, 0], axis=0), jnp.take(ys, pos2[:, 1], axis=0)


C_HEAD = 128
HGRN_LEVELS = (1, 2, 4, 8, 16, 32)


def _cumsum_rows(x, row):
    axis = x.ndim - 2
    s = 1
    while s < x.shape[axis]:
        x = x + jnp.where(row >= s, pltpu.roll(x, s, axis=axis), 0.0)
        s *= 2
    return x


def _anchor_rows(b, m):
    n = b.shape[0]
    if m >= 8:
        parts = [jnp.broadcast_to(b[base + m:base + m + 1, :], (2 * m, b.shape[1]))
                 for base in range(0, n, 2 * m)]
        return parts[0] if len(parts) == 1 else jnp.concatenate(parts, axis=0)
    b3 = b.reshape(n // 8, 8, b.shape[1])
    sub = lax.broadcasted_iota(jnp.int32, b3.shape, 1)
    out = None
    for base in range(8 - 2 * m, -1, -2 * m):
        mid = jnp.broadcast_to(b3[:, base + m:base + m + 1, :], b3.shape)
        out = mid if out is None else jnp.where(sub < base + 2 * m, mid, out)
    return out.reshape(n, b.shape[1])


def _hgrn2_body(q_ref, f_ref, i_ref, g_ref, loglb_ref, log1mlb_ref, omlb_ref, gain_ref,
                o_ref, st_ref, *, hb, n_chunk):
    @pl.when(pl.program_id(2) == 0)
    def _():
        st_ref[...] = jnp.zeros_like(st_ref)

    row = lax.broadcasted_iota(jnp.int32, (CHUNK, hb * C_HEAD), 0)
    tt = lax.broadcasted_iota(jnp.int32, (CHUNK, CHUNK), 0)
    ss = lax.broadcasted_iota(jnp.int32, (CHUNK, CHUNK), 1)
    txs = tt ^ ss
    lvl_id = jnp.where(tt == ss, 0, -1)
    for li, m in enumerate(HGRN_LEVELS):
        lvl_id = jnp.where((ss < tt) & (txs >= m) & (txs < 2 * m), li + 1, lvl_id)

    heads = [slice(h * C_HEAD, (h + 1) * C_HEAD) for h in range(hb)]

    def head_scores(qm, km):
        return jnp.stack([_dot_nt(qm[:, hs], km[:, hs]) for hs in heads])

    def chunk_step(c, carry):
        rs = pl.ds(pl.multiple_of(c * CHUNK, CHUNK), CHUNK)
        q = q_ref[rs, :]
        f = f_ref[rs, :]
        v = i_ref[rs, :].astype(BF16)
        g = g_ref[rs, :]
        qs = q * jax.nn.sigmoid(q)
        log_sig = jnp.minimum(f, 0.0) - jnp.log1p(jnp.exp(-jnp.abs(f)))
        y = log1mlb_ref[...] + log_sig
        x = loglb_ref[...]
        log_f = jnp.maximum(x, y) + jnp.log1p(jnp.exp(-jnp.abs(x - y)))
        key = omlb_ref[...] * jax.nn.sigmoid(-f)
        b = _cumsum_rows(log_f, row)
        scores = jnp.where(lvl_id == 0, head_scores(qs.astype(BF16), key.astype(BF16)), 0.0)
        for li, m in enumerate(HGRN_LEVELS):
            upper = (row & m) != 0
            e = jnp.exp(jnp.where(upper, 1.0, -1.0) * (b - _anchor_rows(b, m)))
            xm = (jnp.where(upper, qs, key) * e).astype(BF16)
            scores = jnp.where(lvl_id == li + 1, head_scores(xm, xm), scores)
        scores = scores.astype(BF16)
        q_dec = (qs * jnp.exp(b)).astype(BF16)
        b_last = b[CHUNK - 1:CHUNK, :]
        k_dec = (key * jnp.exp(b_last - b)).astype(BF16)
        s_dec = jnp.exp(b_last)
        outs = []
        for h, hs in enumerate(heads):
            st = st_ref[h]
            o = _dot(scores[h], v[:, hs]) + _dot_nt(q_dec[:, hs], st.astype(BF16))
            st_ref[h] = st * s_dec[:, hs] + _dot_tn(v[:, hs], k_dec[:, hs])
            outs.append(o * lax.rsqrt(jnp.mean(o * o, axis=-1, keepdims=True) + RMS_EPS))
        o = jnp.concatenate(outs, axis=1) * gain_ref[...] * (g * jax.nn.sigmoid(g))
        o_ref[rs, :] = o.astype(o_ref.dtype)
        return carry

    lax.fori_loop(0, n_chunk, chunk_step, 0)


def hgrn2(cols, lb, norm_gain, bsz, seq, hb=8, tc=512):
    n, c4 = cols.shape
    c = c4 // 4
    wl = hb * C_HEAD
    nhb = c // wl
    nt = seq // tc
    lb = lb.astype(F32).reshape(1, c)
    loglb = jnp.log(lb)
    log1mlb = jnp.log1p(-lb)
    omlb = 1.0 - lb

    def col(k):
        return pl.BlockSpec((tc, wl), lambda b, h, t, k=k: (b * nt + t, k * nhb + h))

    vec = pl.BlockSpec((1, wl), lambda b, h, t: (0, h))
    return pl.pallas_call(
        functools.partial(_hgrn2_body, hb=hb, n_chunk=tc // CHUNK),
        grid=(bsz, nhb, nt),
        in_specs=[col(0), col(1), col(2), col(3), vec, vec, vec, vec],
        out_specs=pl.BlockSpec((tc, wl), lambda b, h, t: (b * nt + t, h)),
        out_shape=jax.ShapeDtypeStruct((n, c), BF16),
        scratch_shapes=[pltpu.VMEM((hb, C_HEAD, C_HEAD), F32)],
        compiler_params=_cparams(("parallel", "parallel", "arbitrary")),
        name="hgrn2",
    )(cols, cols, cols, cols, loglb, log1mlb, omlb, norm_gain.reshape(1, c))


B_WIDTH = 2048
B_HEAD = 64
B_HEADS = B_WIDTH // B_HEAD
B_LORA_PAD = 128
B_GATE_LORA = 256
B_COLS_PAD = 3 * B_WIDTH + 2 * B_LORA_PAD + B_GATE_LORA


def _head_pair_ones():
    r = lax.broadcasted_iota(jnp.int32, (LANES, LANES), 0) // B_HEAD
    c = lax.broadcasted_iota(jnp.int32, (LANES, LANES), 1) // B_HEAD
    return (r == c).astype(BF16)


def _head_sums(x, ones):
    outs = []
    for j in range(x.shape[1] // LANES):
        hi, lo = _split_bf16(x[:, j * LANES:(j + 1) * LANES])
        outs.append(_dot(hi, ones) + _dot(lo, ones))
    return outs[0] if len(outs) == 1 else jnp.concatenate(outs, axis=1)


def _dot3(a, wh, wl):
    ah, al = _split_bf16(a)
    return _dot(ah, wh) + (_dot(al, wh) + _dot(ah, wl))


def _softplus(z):
    return jnp.maximum(z, 0.0) + jnp.log1p(jnp.exp(-jnp.abs(z)))


def _rwkv_prep_body(c_ref, p_ref, mu_ref, w0_ref, a0_ref, kk_ref, ka_ref,
                    wuh_ref, wul_ref, auh_ref, aul_ref, gu_ref,
                    r_o, lw_o, k_o, v_o, kk_o, a_o, g_o, *, tiles_per_seq):
    i = pl.program_id(0)
    x = c_ref[...]
    tm = x.shape[0]
    row = lax.broadcasted_iota(jnp.int32, x.shape, 0)
    prev = jnp.where(i % tiles_per_seq == 0, 0.0, p_ref[7:8, :])
    shifted = jnp.where(row == 0, prev, pltpu.roll(x, 1, axis=0))
    mixed = x + (shifted - x) * mu_ref[...]
    w = B_WIDTH
    r = mixed[:, 0:w]
    k = mixed[:, w:2 * w]
    v = mixed[:, 2 * w:3 * w]
    wd = mixed[:, 3 * w:3 * w + B_LORA_PAD]
    ad = mixed[:, 3 * w + B_LORA_PAD:3 * w + 2 * B_LORA_PAD]
    gd = mixed[:, 3 * w + 2 * B_LORA_PAD:]
    wlog = -_softplus(-(w0_ref[...] + _dot3(jnp.tanh(wd), wuh_ref[...], wul_ref[...]))) - 0.5
    a = jax.nn.sigmoid(a0_ref[...] + _dot3(ad, auh_ref[...], aul_ref[...]))
    g = _dot(jax.nn.sigmoid(gd).astype(BF16), gu_ref[...])
    kk = k * kk_ref[...]
    ssq = _head_sums(kk * kk, _head_pair_ones())
    kk = kk / jnp.maximum(jnp.sqrt(ssq), 1e-12)
    r_o[...] = r
    lw_o[...] = -jnp.exp(wlog)
    k_o[...] = k * (1.0 + (a - 1.0) * ka_ref[...])
    v_o[...] = v
    kk_o[...] = kk
    a_o[...] = a
    g_o[...] = g


def _pad_rows(w, rows):
    return jnp.zeros((rows, w.shape[1]), w.dtype).at[:w.shape[0]].set(w)


def rwkv_prep(cols, mu_pad, w0, w_up, a0, a_up, g_up, k_k, k_a, seq, tm=256):
    n, cw = cols.shape
    w = B_WIDTH
    wuh, wul = _split_bf16(_pad_rows(w_up, B_LORA_PAD))
    auh, aul = _split_bf16(_pad_rows(a_up, B_LORA_PAD))
    row = pl.BlockSpec((tm, w), lambda i: (i, 0))
    vec = pl.BlockSpec((1, w), lambda i: (0, 0))
    lora = pl.BlockSpec((B_LORA_PAD, w), lambda i: (0, 0))
    outs = pl.pallas_call(
        functools.partial(_rwkv_prep_body, tiles_per_seq=seq // tm),
        grid=(n // tm,),
        in_specs=[pl.BlockSpec((tm, cw), lambda i: (i, 0)),
                  pl.BlockSpec((8, cw), lambda i: (jnp.maximum(i * (tm // 8) - 1, 0), 0)),
                  pl.BlockSpec((1, cw), lambda i: (0, 0)),
                  vec, vec, vec, vec, lora, lora, lora, lora,
                  pl.BlockSpec((B_GATE_LORA, w), lambda i: (0, 0))],
        out_specs=[row] * 7,
        out_shape=[jax.ShapeDtypeStruct((n, w), F32)] * 7,
        compiler_params=_cparams(("parallel",)),
        name="rwkv_prep",
    )(cols, cols, mu_pad.reshape(1, cw), w0.reshape(1, w), a0.reshape(1, w), k_k.reshape(1, w),
      k_a.reshape(1, w), wuh, wul, auh, aul, g_up.astype(BF16))
    return outs


def _rwkv_scan_body(r_ref, lw_ref, k_ref, v_ref, kk_ref, a_ref, y_ref, st_ref, qm_ref, ys_ref, *, hb, n_chunk):
    @pl.when(pl.program_id(2) == 0)
    def _():
        st_ref[...] = jnp.zeros_like(st_ref)

    L = CHUNK
    row = lax.broadcasted_iota(jnp.int32, (hb, L, B_HEAD), 1)
    tt = lax.broadcasted_iota(jnp.int32, (L, L), 0)
    ss = lax.broadcasted_iota(jnp.int32, (L, L), 1)
    strict = ss < tt
    incl = ss <= tt
    eye = ss == tt
    eye_f = eye.astype(F32)

    def local_step(c, carry):
        sl = pl.ds(pl.multiple_of(c * L, L), L)
        r = r_ref[:, sl, :]
        lw = lw_ref[:, sl, :]
        k = k_ref[:, sl, :]
        v = v_ref[:, sl, :]
        kk = kk_ref[:, sl, :]
        a = a_ref[:, sl, :]
        lp = _cumsum_rows(lw, row)
        p = jnp.exp(lp)
        inv_p = jnp.exp(-lp)
        at = -kk * jnp.exp(lp - lw)
        bt = kk * a * inv_p
        kt = k * inv_p
        rt = r * p
        p_last = p[:, L - 1:L, :]
        bp = bt * p_last
        kp = kt * p_last
        ar = jnp.concatenate([at, rt], axis=1)
        m_b = _bdot_nt(ar, bt)
        m_k = _bdot_nt(ar, kt)
        nmat = jnp.where(strict, m_b[:, :L], 0.0)
        a_rb = jnp.where(incl, m_b[:, L:], 0.0)
        g = _bdot(jnp.where(strict, m_k[:, :L], 0.0), v)
        tmat = eye_f + nmat
        npow = nmat
        lvl = 2
        while lvl < L:
            npow = _bdot(npow, npow)
            tmat = tmat + _bdot(npow, tmat)
            lvl *= 2
        w = _bdot(tmat, at)
        u = _bdot(tmat, g)
        qe = rt + _bdot(a_rb, w)
        yl = _bdot(a_rb, u) + _bdot(jnp.where(incl, m_k[:, L:], 0.0), v)
        qm_ref[:, c, 0:L, :] = qe
        ys_ref[:, c, 0:L, :] = yl
        for h in range(hb):
            qm_ref[h, c, L:2 * L, :] = _dot_tn(bp[h], w[h]) + jnp.where(eye, p_last[h], 0.0)
            ys_ref[h, c, L:2 * L, :] = _dot_tn(bp[h], u[h]) + _dot_tn(kp[h], v[h])
        return carry

    lax.fori_loop(0, n_chunk, local_step, 0)

    def serial_step(c, carry):
        sl = pl.ds(pl.multiple_of(c * L, L), L)
        out = _bdot(qm_ref[:, c], st_ref[...]) + ys_ref[:, c]
        y_ref[:, sl, :] = out[:, :L]
        st_ref[...] = out[:, L:]
        return carry

    lax.fori_loop(0, n_chunk, serial_step, 0)


def rwkv_scan(r, lw, k, v, kk, a, bsz, seq, hb=16, tc=256):
    nh, n, hd = r.shape
    tc = min(tc, seq)
    nt = seq // tc
    n_chunk = tc // CHUNK
    blk = pl.BlockSpec((hb, tc, hd), lambda b, h, t: (h, b * nt + t, 0))
    return pl.pallas_call(
        functools.partial(_rwkv_scan_body, hb=hb, n_chunk=n_chunk),
        grid=(bsz, nh // hb, nt),
        in_specs=[blk] * 6,
        out_specs=blk,
        out_shape=jax.ShapeDtypeStruct((nh, n, hd), F32),
        scratch_shapes=[pltpu.VMEM((hb, hd, hd), F32),
                        pltpu.VMEM((hb, n_chunk, 2 * CHUNK, hd), F32),
                        pltpu.VMEM((hb, n_chunk, 2 * CHUNK, hd), F32)],
        compiler_params=_cparams(("parallel", "parallel", "arbitrary")),
        name="rwkv_scan",
    )(r, lw, k, v, kk, a)


def _rwkv_post_body(y_ref, r_ref, k_ref, v_ref, g_ref, rk_ref, lg_ref, lb_ref, o_ref):
    ones = _head_pair_ones()
    y = y_ref[...]
    mean = _head_sums(y, ones) * (1.0 / B_HEAD)
    yc = y - mean
    var = _head_sums(yc * yc, ones) * (1.0 / B_HEAD)
    yn = yc * lax.rsqrt(var + GN_EPS) * lg_ref[...] + lb_ref[...]
    v = v_ref[...]
    bonus = _head_sums(r_ref[...] * k_ref[...] * rk_ref[...], ones) * v
    o_ref[...] = ((yn + bonus) * g_ref[...]).astype(o_ref.dtype)


def rwkv_post(y, r, k, v, g, r_k, ln_gain, ln_bias, tm=256):
    n, w = y.shape
    row = pl.BlockSpec((tm, w), lambda i: (i, 0))
    vec = pl.BlockSpec((1, w), lambda i: (0, 0))
    return pl.pallas_call(
        _rwkv_post_body,
        grid=(n // tm,),
        in_specs=[row] * 5 + [vec] * 3,
        out_specs=row,
        out_shape=jax.ShapeDtypeStruct((n, w), BF16),
        compiler_params=_cparams(("parallel",)),
        name="rwkv_post",
    )(y, r, k, v, g, r_k.reshape(1, w), ln_gain.reshape(1, w), ln_bias.reshape(1, w))


def _to_heads(z):
    n = z.shape[0]
    return z.reshape(n, B_HEADS, B_HEAD).transpose(1, 0, 2)


def mixer_rwkv7(cols_b, mu_pad, w0, w_up, a0, a_up, g_up, k_k, k_a, r_k, ln_gain, ln_bias, bsz, seq):
    r, lw, k, v, kk, a, g = rwkv_prep(cols_b, mu_pad, w0, w_up, a0, a_up, g_up, k_k, k_a, seq)
    yh = rwkv_scan(*[_to_heads(z) for z in (r, lw, k, v, kk, a)], bsz, seq)
    y = yh.transpose(1, 0, 2).reshape(r.shape)
    return rwkv_post(y, r, k, v, g, r_k.reshape(-1), ln_gain, ln_bias)


A_Q_RANK = 768
A_KV_RANK = 512
IDX_DIM = 64
IDX_HEADS = 16
A_HEADS = 16
A_HEAD_DIM = 128
A_COLS_PAD = A_Q_RANK + A_KV_RANK + 2 * LANES
INT_MIN = -2 ** 31


def _dsa_prep_body(c_ref, qn_ref, kn_ref, g_ref, b_ref, q_o, c_o, k_o, w_o):
    x = c_ref[...]
    ql = x[:, :A_Q_RANK]
    q_o[...] = (ql * lax.rsqrt(jnp.mean(ql * ql, axis=-1, keepdims=True) + RMS_EPS) * qn_ref[...]).astype(q_o.dtype)
    kv = x[:, A_Q_RANK:A_Q_RANK + A_KV_RANK]
    c_o[...] = (kv * lax.rsqrt(jnp.mean(kv * kv, axis=-1, keepdims=True) + RMS_EPS) * kn_ref[...]).astype(c_o.dtype)
    o = A_Q_RANK + A_KV_RANK
    ki = x[:, o:o + LANES]
    valid = lax.broadcasted_iota(jnp.int32, ki.shape, 1) < IDX_DIM
    mu = jnp.sum(jnp.where(valid, ki, 0.0), axis=-1, keepdims=True) * (1.0 / IDX_DIM)
    kc = jnp.where(valid, ki - mu, 0.0)
    var = jnp.sum(kc * kc, axis=-1, keepdims=True) * (1.0 / IDX_DIM)
    kn = kc * lax.rsqrt(var + LN_EPS) * g_ref[...] + b_ref[...]
    k_o[...] = kn[:, :IDX_DIM].astype(k_o.dtype)
    w_o[...] = x[:, o + LANES:o + 2 * LANES] * (IDX_HEADS ** -0.5 * IDX_DIM ** -0.5)


def dsa_prep(cols, q_norm, kv_norm, k_gain, k_bias, tm=512):
    n, cw = cols.shape
    pad = jnp.zeros((LANES - IDX_DIM,), F32)
    return pl.pallas_call(
        _dsa_prep_body,
        grid=(n // tm,),
        in_specs=[pl.BlockSpec((tm, cw), lambda i: (i, 0)),
                  pl.BlockSpec((1, A_Q_RANK), lambda i: (0, 0)),
                  pl.BlockSpec((1, A_KV_RANK), lambda i: (0, 0)),
                  pl.BlockSpec((1, LANES), lambda i: (0, 0)),
                  pl.BlockSpec((1, LANES), lambda i: (0, 0))],
        out_specs=[pl.BlockSpec((tm, A_Q_RANK), lambda i: (i, 0)),
                   pl.BlockSpec((tm, A_KV_RANK), lambda i: (i, 0)),
                   pl.BlockSpec((tm, IDX_DIM), lambda i: (i, 0)),
                   pl.BlockSpec((tm, LANES), lambda i: (i, 0))],
        out_shape=[jax.ShapeDtypeStruct((n, A_Q_RANK), BF16), jax.ShapeDtypeStruct((n, A_KV_RANK), BF16),
                   jax.ShapeDtypeStruct((n, IDX_DIM), BF16), jax.ShapeDtypeStruct((n, LANES), F32)],
        compiler_params=_cparams(("parallel",)),
        name="dsa_prep",
    )(cols, q_norm.reshape(1, -1), kv_norm.reshape(1, -1),
      jnp.concatenate([k_gain, pad]).reshape(1, LANES), jnp.concatenate([k_bias, pad]).reshape(1, LANES))


def _head_mm_body(a_ref, w_ref, o_ref, *, scale):
    o_ref[0] = (_dot(a_ref[...], w_ref[0]) * scale).astype(o_ref.dtype)


def head_matmul(a, w, scale, tm=1024):
    n = a.shape[0]
    nh, k, f = w.shape
    tm = min(tm, n)
    return pl.pallas_call(
        functools.partial(_head_mm_body, scale=scale),
        grid=(nh, n // tm),
        in_specs=[pl.BlockSpec((tm, k), lambda h, i: (i, h)),
                  pl.BlockSpec((1, k, f), lambda h, i: (h, 0, 0))],
        out_specs=pl.BlockSpec((1, tm, f), lambda h, i: (h, i, 0)),
        out_shape=jax.ShapeDtypeStruct((nh, n, f), BF16),
        compiler_params=_cparams(("parallel", "parallel")),
        name="head_matmul",
    )(a, w)


DSA_KT = 256
DSA_AT = 512
DSA_HEAD_GROUP = 4


def _dsa_attend_body(qi_ref, wi_ref, qa_ref, ck_ref, kx_ref, wuv_ref, sl_ref, o_ref,
                     keys_ref, m_ref, l_ref, acc_ref, *, top_k, seq):
    j = pl.program_id(1)
    at_n = ((j + 1) * Q_BLOCK + DSA_AT - 1) // DSA_AT
    kt_n = at_n * (DSA_AT // DSA_KT)
    nq = Q_BLOCK
    nh = A_HEADS
    qrow = lax.broadcasted_iota(jnp.int32, (nq, 1), 0)
    qpos = j * Q_BLOCK + qrow
    limit = (qpos // CHUNK + 1) * CHUNK
    lane = lax.broadcasted_iota(jnp.int32, (nq, DSA_KT), 1)
    lane_a = lax.broadcasted_iota(jnp.int32, (nq, DSA_AT), 1)

    qi = qi_ref[...].reshape(nh * nq, IDX_DIM)
    wi = wi_ref[...]

    def score_tile(kt, carry):
        starts = [pl.multiple_of(kt * DSA_AT + t * DSA_KT, DSA_KT) for t in range(DSA_AT // DSA_KT)]
        rels = [_dot_nt(qi, kx_ref[pl.ds(k0, DSA_KT), :]) for k0 in starts]
        for k0, rel in zip(starts, rels):
            score = jnp.sum(jnp.maximum(rel, 0.0).reshape(nh, nq, DSA_KT) * wi, axis=0)
            bits = lax.bitcast_convert_type(score, jnp.int32)
            key = bits ^ ((bits >> 31) & 0x7FFFFFFF)
            keys_ref[:, pl.ds(k0, DSA_KT)] = jnp.where(k0 + lane < limit, key, INT_MIN)
        return carry

    lax.fori_loop(0, at_n, score_tile, 0)

    def count_keys(pred):
        def count_tile(kt, cnt):
            k0 = pl.multiple_of(kt * DSA_AT, DSA_AT)
            hit = pred(keys_ref[:, pl.ds(k0, DSA_AT)], k0).astype(jnp.int32)
            for t in range(DSA_AT // LANES):
                cnt = cnt + hit[:, t * LANES:(t + 1) * LANES]
            return cnt

        cnt = lax.fori_loop(0, at_n, count_tile, jnp.zeros((nq, LANES), jnp.int32))
        return jnp.sum(cnt, axis=-1, keepdims=True)

    def bit_step(i, carry):
        ans, n_ge = carry
        cand = ans | lax.shift_left(jnp.int32(1), 31 - i)
        cand_s = cand ^ INT_MIN
        total = count_keys(lambda key, k0: key >= cand_s)
        ok = total >= top_k
        return jnp.where(ok, cand, ans), jnp.where(ok, total, n_ge)

    zero = jnp.zeros((nq, 1), jnp.int32)
    ans, n_ge = lax.fori_loop(0, 32, bit_step, (zero, zero))
    thr = ans ^ INT_MIN

    excess = (ans != 0) & (n_ge > top_k)
    idx_bits = max(1, (seq - 1).bit_length())

    def tie_cut():
        need = top_k - count_keys(lambda key, k0: key > thr)

        def idx_step(i, cut):
            cand = cut | lax.shift_left(jnp.int32(1), idx_bits - 1 - i)
            below = count_keys(lambda key, k0: (key == thr) & (k0 + lane_a < cand))
            return jnp.where(below < need, cand, cut)

        return lax.fori_loop(0, idx_bits, idx_step, zero)

    cut = lax.cond(jnp.max(excess.astype(jnp.int32)) > 0, tie_cut, lambda: zero)
    cut = jnp.where(excess, cut, seq)

    m_ref[...] = jnp.full_like(m_ref, -jnp.inf)
    l_ref[...] = jnp.zeros_like(l_ref)
    acc_ref[...] = jnp.zeros_like(acc_ref)
    slopes = sl_ref[...]
    gh = DSA_HEAD_GROUP

    def attend_tile(kt, carry):
        k0 = pl.multiple_of(kt * DSA_AT, DSA_AT)
        ck = ck_ref[pl.ds(k0, DSA_AT), :]
        key = keys_ref[:, pl.ds(k0, DSA_AT)]
        kidx = k0 + lane_a
        keep = ((key > thr) | ((key == thr) & (kidx <= cut))) & (key > INT_MIN)
        mask_bias = jnp.where(keep, 0.0, -jnp.inf)
        neg_dist = -jnp.abs(qpos - kidx).astype(F32)
        def qk(g):
            return _dot_nt(qa_ref[g * gh:(g + 1) * gh].reshape(gh * nq, A_KV_RANK), ck)

        s_next = qk(0)
        for g in range(nh // gh):
            s = s_next
            if g + 1 < nh // gh:
                s_next = qk(g + 1)
            rows = pl.ds(g * gh * nq, gh * nq)
            s = s.reshape(gh, nq, DSA_AT)
            s = (s + (slopes[g * gh:(g + 1) * gh] * neg_dist + mask_bias)).reshape(gh * nq, DSA_AT)
            m_old = m_ref[rows, :]
            m_new = jnp.maximum(m_old, jnp.max(s, axis=-1, keepdims=True))
            m_safe = jnp.where(m_new == -jnp.inf, 0.0, m_new)
            p = jnp.exp(s - m_safe)
            alpha = jnp.exp(m_old - m_safe)
            l_ref[rows, :] = alpha * l_ref[rows, :] + jnp.sum(p, axis=-1, keepdims=True)
            acc_ref[rows, :] = alpha * acc_ref[rows, :] + _dot(p.astype(BF16), ck)
            m_ref[rows, :] = m_new
        return carry

    lax.fori_loop(0, at_n, attend_tile, 0)
    o_lat = (acc_ref[...] / l_ref[...]).astype(BF16).reshape(nh, nq, A_KV_RANK)
    for h in range(nh):
        o_ref[:, h * A_HEAD_DIM:(h + 1) * A_HEAD_DIM] = _dot(o_lat[h], wuv_ref[h]).astype(o_ref.dtype)


def dsa_attend(qi, wi, qa, ckv, kx, w_uv, bsz, seq):
    n = ckv.shape[0]
    nblk = seq // Q_BLOCK
    top_k = min(TOPK_MAX, seq // 4)
    start = 2.0 ** (-8.0 / A_HEADS)
    slopes = jnp.asarray([start ** (h + 1) for h in range(A_HEADS)], F32).reshape(A_HEADS, 1, 1)

    def qblk(last):
        return pl.BlockSpec((A_HEADS, Q_BLOCK, last), lambda b, j: (0, b * nblk + j, 0))

    return pl.pallas_call(
        functools.partial(_dsa_attend_body, top_k=top_k, seq=seq),
        grid=(bsz, nblk),
        in_specs=[qblk(IDX_DIM), qblk(1), qblk(A_KV_RANK),
                  pl.BlockSpec((seq, A_KV_RANK), lambda b, j: (b, 0)),
                  pl.BlockSpec((seq, IDX_DIM), lambda b, j: (b, 0)),
                  pl.BlockSpec((A_HEADS, A_KV_RANK, A_HEAD_DIM), lambda b, j: (0, 0, 0)),
                  pl.BlockSpec((A_HEADS, 1, 1), lambda b, j: (0, 0, 0))],
        out_specs=pl.BlockSpec((Q_BLOCK, A_HEADS * A_HEAD_DIM), lambda b, j: (b * nblk + j, 0)),
        out_shape=jax.ShapeDtypeStruct((n, A_HEADS * A_HEAD_DIM), BF16),
        scratch_shapes=[pltpu.VMEM((Q_BLOCK, seq), jnp.int32),
                        pltpu.VMEM((A_HEADS * Q_BLOCK, 1), F32),
                        pltpu.VMEM((A_HEADS * Q_BLOCK, 1), F32),
                        pltpu.VMEM((A_HEADS * Q_BLOCK, A_KV_RANK), F32)],
        compiler_params=_cparams(("parallel", "arbitrary")),
        name="dsa_attend",
    )(qi, wi, qa, ckv, kx, w_uv, slopes)


def mixer_dsa(cols_a, q_norm, kv_norm, w_uq, w_uk, w_uv, w_idx_q, k_gain, k_bias, bsz, seq):
    n = cols_a.shape[0]
    q_lat, c_kv, k_idx, w_idx = dsa_prep(cols_a, q_norm, kv_norm, k_gain, k_bias)
    q = matmul(q_lat, w_uq.reshape(A_Q_RANK, -1).astype(BF16), out_dtype=BF16)
    qa = head_matmul(q, w_uk.transpose(1, 2, 0).astype(BF16), A_HEAD_DIM ** -0.5)
    qi = matmul(q_lat, w_idx_q.reshape(A_Q_RANK, -1).astype(BF16), out_dtype=BF16)
    qi = qi.reshape(n, IDX_HEADS, IDX_DIM).transpose(1, 0, 2)
    wi = w_idx[:, :IDX_HEADS].T.reshape(IDX_HEADS, n, 1)
    return dsa_attend(qi, wi, qa, c_kv, k_idx, w_uv.transpose(1, 0, 2).astype(BF16), bsz, seq)


A_SPLITS = (A_Q_RANK, A_KV_RANK, IDX_DIM, IDX_HEADS)
A_COLS = sum(A_SPLITS)
DECAY_LORA = 96
AAA_LORA = 96


def _pad_cols(w, sizes, padded):
    out, o = [], 0
    for s, p in zip(sizes, padded):
        piece = w[..., o:o + s]
        if p > s:
            piece = jnp.concatenate([piece, jnp.zeros(w.shape[:-1] + (p - s,), w.dtype)], axis=-1)
        out.append(piece)
        o += s
    return jnp.concatenate(out, axis=-1)


def kernel(x, ln1_gain, ln1_bias, ln2_gain, ln2_bias, w_in_even, a_q_norm, a_kv_norm, a_w_uq, a_w_uk, a_w_uv, a_w_idx_q, a_idx_k_gain, a_idx_k_bias, b_mu, b_w0, b_w_up, b_a0, b_a_up, b_g_up, b_k_k, b_k_a, b_r_k, b_ln_gain, b_ln_bias, w_out_even, w_in_odd, c_lb_logits, c_norm_gain, w_out_odd, router_group, router_group_bias, router_expert, router_expert_bias, moe_w_gate, moe_w_up, moe_w_down):
    bsz, seq, d = x.shape
    n = bsz * seq
    lb_table = jnp.cumsum(jax.nn.softmax(c_lb_logits.astype(F32), axis=0), axis=0)
    lb_table = lb_table - lb_table[:1]
    a_sizes, a_padded = A_SPLITS, (A_Q_RANK, A_KV_RANK, LANES, LANES)
    b_sizes = (3 * B_WIDTH, DECAY_LORA, AAA_LORA, B_GATE_LORA)
    b_padded = (3 * B_WIDTH, B_LORA_PAD, B_LORA_PAD, B_GATE_LORA)
    xf = x.reshape(n, d)
    xb = xf.astype(BF16)
    for layer in range(DEPTH):
        j = layer // 2
        if layer % 2 == 0:
            w_in = w_in_even[j]
            w_a = _pad_cols(w_in[:, :A_COLS], a_sizes, a_padded).astype(BF16)
            w_b = _pad_cols(w_in[:, A_COLS:], b_sizes, b_padded).astype(BF16)
            cols_a = matmul(xb, w_a, tn=512)
            cols_b = matmul(xb, w_b, tn=512)
            y_a = mixer_dsa(cols_a, a_q_norm[j], a_kv_norm[j], a_w_uq[j], a_w_uk[j], a_w_uv[j],
                            a_w_idx_q[j], a_idx_k_gain[j], a_idx_k_bias[j], bsz, seq)
            y_b = mixer_rwkv7(cols_b, _pad_cols(b_mu[j], b_sizes, b_padded), b_w0[j], b_w_up[j], b_a0[j],
                              b_a_up[j], b_g_up[j], b_k_k[j], b_k_a[j], b_r_k[j], b_ln_gain[j], b_ln_bias[j],
                              bsz, seq)
            mixed = jnp.concatenate([y_a, y_b], axis=-1)
            w_out = w_out_even[j]
        else:
            cols = matmul(xb, w_in_odd[j].astype(BF16))
            mixed = hgrn2(cols, lb_table[j], c_norm_gain[j], bsz, seq)
            w_out = w_out_odd[j]
        xf, xb = proj_residual_ln(mixed, w_out.astype(BF16), xf, ln1_gain[layer], ln1_bias[layer])
        y0, y1 = hier_moe(xf, xb, router_group[layer], router_group_bias[layer], router_expert[layer],
                          router_expert_bias[layer], moe_w_gate, moe_w_up, moe_w_down, layer)
        xf, xb = add_ln(xf, y0, y1, ln2_gain[layer], ln2_bias[layer])
    return xf.reshape(bsz, seq, d)
```

```python
import functools

import jax
import jax.numpy as jnp
import numpy as np
from jax import lax
from jax.experimental import pallas as pl
from jax.experimental.pallas import tpu as pltpu

F32 = jnp.float32
BF16 = jnp.bfloat16

DEPTH = 4
ALPHA = (2 * DEPTH) ** 0.25
LN_EPS = 1e-5
RMS_EPS = 1e-6
GN_EPS = 64e-5
CHUNK = 64
Q_BLOCK = 128
TOPK_MAX = 256
N_GROUPS = 4
EXPERTS_PER_GROUP = 8
N_EXPERTS = N_GROUPS * EXPERTS_PER_GROUP

LANES = 128
VMEM_LIMIT = 56 * 1024 * 1024


def _cparams(sem):
    return pltpu.CompilerParams(dimension_semantics=sem, vmem_limit_bytes=VMEM_LIMIT)


def _dot(a, b):
    return jnp.dot(a, b, preferred_element_type=F32)


def _dot_nt(a, b):
    return lax.dot_general(a, b, (((1,), (1,)), ((), ())), preferred_element_type=F32)


def _dot_tn(a, b):
    return lax.dot_general(a, b, (((0,), (0,)), ((), ())), preferred_element_type=F32)


def _bdot(a, b):
    return lax.dot_general(a, b, (((2,), (1,)), ((0,), (0,))), preferred_element_type=F32)


def _bdot_nt(a, b):
    return lax.dot_general(a, b, (((2,), (2,)), ((0,), (0,))), preferred_element_type=F32)


def _split_bf16(x):
    hi = x.astype(BF16)
    lo = (x - hi.astype(F32)).astype(BF16)
    return hi, lo


def _mm_body(a_ref, b_ref, o_ref):
    o_ref[...] = _dot(a_ref[...].astype(BF16), b_ref[...]).astype(o_ref.dtype)


def matmul(a, b, out_dtype=F32, tm=512, tn=1024):
    m, k = a.shape
    _, n = b.shape
    tm = min(tm, m)
    tn = min(tn, n)
    assert m % tm == 0 and n % tn == 0
    return pl.pallas_call(
        _mm_body,
        grid=(n // tn, m // tm),
        in_specs=[pl.BlockSpec((tm, k), lambda j, i: (i, 0)),
                  pl.BlockSpec((k, tn), lambda j, i: (0, j))],
        out_specs=pl.BlockSpec((tm, tn), lambda j, i: (i, j)),
        out_shape=jax.ShapeDtypeStruct((m, n), out_dtype),
        compiler_params=_cparams(("parallel", "parallel")),
        name="matmul",
    )(a, b)


def _mm_wcast_body(a_ref, w_ref, o_ref, wb_ref):
    @pl.when(pl.program_id(1) == 0)
    def _():
        wb_ref[...] = w_ref[0].astype(BF16)

    o_ref[...] = _dot(a_ref[...], wb_ref[...]).astype(o_ref.dtype)


def matmul_stacked_w(a, w, layer, out_dtype=F32, tm=512, tn=1024):
    m, k = a.shape
    n = w.shape[2]
    tm = min(tm, m)
    return pl.pallas_call(
        _mm_wcast_body,
        grid=(n // tn, m // tm),
        in_specs=[pl.BlockSpec((tm, k), lambda j, i: (i, 0)),
                  pl.BlockSpec((1, k, tn), lambda j, i: (layer, 0, j), pipeline_mode=pl.Buffered(1))],
        out_specs=pl.BlockSpec((tm, tn), lambda j, i: (i, j)),
        out_shape=jax.ShapeDtypeStruct((m, n), out_dtype),
        scratch_shapes=[pltpu.VMEM((k, tn), BF16)],
        compiler_params=_cparams(("parallel", "arbitrary")),
        name="matmul_stacked_w",
    )(a, w)


def _ln_rows(z, gain, bias):
    mu = jnp.mean(z, axis=-1, keepdims=True)
    zc = z - mu
    var = jnp.mean(zc * zc, axis=-1, keepdims=True)
    return zc * lax.rsqrt(var + LN_EPS) * gain + bias


def _proj_ln_body(a_ref, w_ref, x_ref, g_ref, b_ref, o_ref, ob_ref, *, nk):
    kk = pl.program_id(1)

    @pl.when(kk == 0)
    def _():
        o_ref[...] = jnp.zeros_like(o_ref)

    o_ref[...] += _dot(a_ref[...].astype(BF16), w_ref[...])

    @pl.when(kk == nk - 1)
    def _():
        y = _ln_rows(ALPHA * x_ref[...] + o_ref[...], g_ref[...], b_ref[...])
        o_ref[...] = y
        ob_ref[...] = y.astype(BF16)


def proj_residual_ln(a, w, x, gain, bias, tm=512, tk=512):
    m, k = a.shape
    d = w.shape[1]
    tm = min(tm, m)
    nk = k // tk
    assert nk >= 2
    return pl.pallas_call(
        functools.partial(_proj_ln_body, nk=nk),
        grid=(m // tm, nk),
        in_specs=[pl.BlockSpec((tm, tk), lambda i, kk: (i, kk)),
                  pl.BlockSpec((tk, d), lambda i, kk: (kk, 0)),
                  pl.BlockSpec((tm, d), lambda i, kk: (i, 0), pipeline_mode=pl.Buffered(1)),
                  pl.BlockSpec((1, d), lambda i, kk: (0, 0)),
                  pl.BlockSpec((1, d), lambda i, kk: (0, 0))],
        out_specs=[pl.BlockSpec((tm, d), lambda i, kk: (i, 0)),
                   pl.BlockSpec((tm, d), lambda i, kk: (i, 0), pipeline_mode=pl.Buffered(1))],
        out_shape=[jax.ShapeDtypeStruct((m, d), F32), jax.ShapeDtypeStruct((m, d), BF16)],
        compiler_params=_cparams(("parallel", "arbitrary")),
        name="proj_residual_ln",
    )(a, w, x, gain.reshape(1, d), bias.reshape(1, d))


def _add_ln_body(x_ref, y0_ref, y1_ref, g_ref, b_ref, o_ref, ob_ref):
    z = ALPHA * x_ref[...] + (y0_ref[...].astype(F32) + y1_ref[...].astype(F32))
    y = _ln_rows(z, g_ref[...], b_ref[...])
    o_ref[...] = y
    ob_ref[...] = y.astype(BF16)


def add_ln(x, y0, y1, gain, bias, tm=256):
    m, d = x.shape
    row = pl.BlockSpec((tm, d), lambda i: (i, 0))
    vec = pl.BlockSpec((1, d), lambda i: (0, 0))
    return pl.pallas_call(
        _add_ln_body,
        grid=(m // tm,),
        in_specs=[row, row, row, vec, vec],
        out_specs=[row, row],
        out_shape=[jax.ShapeDtypeStruct((m, d), F32), jax.ShapeDtypeStruct((m, d), BF16)],
        compiler_params=_cparams(("parallel",)),
        name="add_ln",
    )(x, y0, y1, gain.reshape(1, d), bias.reshape(1, d))


def _router_body(x_ref, wh_ref, wl_ref, b_ref, tri_ref, eid_ref, gate_ref, rank_ref, cnt_ref, base_ref):
    xh, xl = _split_bf16(x_ref[...])
    wh = wh_ref[...]
    logits = _dot(xh, wh) + (_dot(xl, wh) + _dot(xh, wl_ref[...])) + b_ref[...]
    lane = lax.broadcasted_iota(jnp.int32, logits.shape, 1)
    neg = jnp.float32(-jnp.inf)
    is_g = lane < N_GROUPS
    gl = jnp.where(is_g, logits, neg)
    gmax = jnp.max(gl, axis=-1, keepdims=True)
    g_idx = jnp.min(jnp.where(gl == gmax, lane, LANES), axis=-1, keepdims=True)
    g_p = 1.0 / jnp.sum(jnp.exp(gl - gmax), axis=-1, keepdims=True)
    e_lane = lane - N_GROUPS
    in_grp = (e_lane >= g_idx * EXPERTS_PER_GROUP) & (e_lane < (g_idx + 1) * EXPERTS_PER_GROUP)
    el = jnp.where(in_grp, logits, neg)
    emax = jnp.max(el, axis=-1, keepdims=True)
    pe = jnp.exp(el - emax)
    pe = pe / jnp.sum(pe, axis=-1, keepdims=True)
    v1 = jnp.max(pe, axis=-1, keepdims=True)
    i1 = jnp.min(jnp.where(in_grp & (pe == v1), lane, LANES), axis=-1, keepdims=True)
    rest = in_grp & (lane != i1)
    pe2 = jnp.where(rest, pe, -1.0)
    v2 = jnp.max(pe2, axis=-1, keepdims=True)
    i2 = jnp.min(jnp.where(rest & (pe2 == v2), lane, LANES), axis=-1, keepdims=True)
    den = v1 + v2
    eid_ref[...] = jnp.where(lane == 0, i1 - N_GROUPS, i2 - N_GROUPS)
    gate_ref[...] = jnp.where(lane == 0, g_p * (v1 / den), g_p * (v2 / den))

    @pl.when(pl.program_id(0) == 0)
    def _():
        base_ref[...] = jnp.zeros_like(base_ref)

    hit1 = lane == i1
    hit2 = lane == i2
    oh1 = hit1.astype(BF16)
    oh2 = hit2.astype(BF16)
    tri = tri_ref[...]
    cnt1 = jnp.sum(oh1.astype(F32), axis=0, keepdims=True)
    cnt2 = jnp.sum(oh2.astype(F32), axis=0, keepdims=True)
    base = base_ref[...]
    r1 = jnp.sum(jnp.where(hit1, base + _dot(tri, oh1), 0.0), axis=-1, keepdims=True)
    r2 = jnp.sum(jnp.where(hit2, base + cnt1 + _dot(tri, oh2), 0.0), axis=-1, keepdims=True)
    rank_ref[...] = jnp.where(lane == 0, r1, r2).astype(jnp.int32)
    base_ref[...] = base + cnt1 + cnt2
    cnt_ref[...] = (base + cnt1 + cnt2).astype(jnp.int32)


def router(x, wg, bg, we, be, tm=512):
    m, d = x.shape
    tm = min(tm, m)
    w = jnp.zeros((d, LANES), F32).at[:, :N_GROUPS].set(wg).at[:, N_GROUPS:N_GROUPS + N_EXPERTS].set(we)
    b = jnp.zeros((1, LANES), F32).at[0, :N_GROUPS].set(bg).at[0, N_GROUPS:N_GROUPS + N_EXPERTS].set(be)
    wh, wl = _split_bf16(w)
    tri = jnp.tril(jnp.ones((tm, tm), BF16), k=-1)
    row = pl.BlockSpec((tm, LANES), lambda i: (i, 0))
    wspec = pl.BlockSpec((d, LANES), lambda i: (0, 0))
    vec = pl.BlockSpec((1, LANES), lambda i: (0, 0))
    eid, gate, rank, cnt = pl.pallas_call(
        _router_body,
        grid=(m // tm,),
        in_specs=[pl.BlockSpec((tm, d), lambda i: (i, 0)), wspec, wspec, vec,
                  pl.BlockSpec((tm, tm), lambda i: (0, 0))],
        out_specs=[row, row, row, vec],
        out_shape=[jax.ShapeDtypeStruct((m, LANES), jnp.int32), jax.ShapeDtypeStruct((m, LANES), F32),
                   jax.ShapeDtypeStruct((m, LANES), jnp.int32), jax.ShapeDtypeStruct((1, LANES), jnp.int32)],
        scratch_shapes=[pltpu.VMEM((1, LANES), F32)],
        compiler_params=_cparams(("arbitrary",)),
        name="router",
    )(x, wh, wl, b, tri)
    return eid[:, :2], gate[:, :2], rank[:, :2], cnt[0, N_GROUPS:N_GROUPS + N_EXPERTS]


def _expert_body(te_ref, tv_ref, tok_ref, x_hbm, gate_ref, wg_ref, wu_ref, wd_ref, o_ref,
                 wgb_ref, wub_ref, wdb_ref, xbuf_ref, sem_ref):
    i = pl.program_id(0)
    n_tiles = pl.num_programs(0)
    tm = xbuf_ref.shape[1]

    def start_gather(tile, slot):
        def issue(r, carry):
            tok = tok_ref[tile * tm + r]
            pltpu.make_async_copy(x_hbm.at[pl.ds(tok, 1)], xbuf_ref.at[slot, pl.ds(r, 1)],
                                  sem_ref.at[slot]).start()
            return carry

        lax.fori_loop(0, tm, issue, 0, unroll=8)

    @pl.when((i == 0) & (tv_ref[0] > 0))
    def _():
        start_gather(0, 0)

    nxt = jnp.minimum(i + 1, n_tiles - 1)

    @pl.when((i + 1 < n_tiles) & (tv_ref[nxt] > 0))
    def _():
        start_gather(i + 1, (i + 1) % 2)

    @pl.when((i == 0) | (te_ref[i] != te_ref[jnp.maximum(i - 1, 0)]))
    def _():
        wgb_ref[...] = wg_ref[0, 0].astype(BF16)
        wub_ref[...] = wu_ref[0, 0].astype(BF16)
        wdb_ref[...] = wd_ref[0, 0].astype(BF16)

    @pl.when(tv_ref[i] > 0)
    def _():
        slot = i % 2
        pltpu.make_async_copy(x_hbm.at[pl.ds(0, tm)], xbuf_ref.at[slot], sem_ref.at[slot]).wait()
        xs = xbuf_ref[slot].astype(BF16)
        hg = _dot(xs, wgb_ref[...])
        hu = _dot(xs, wub_ref[...])
        act = (hg * jax.nn.sigmoid(hg)) * hu * gate_ref[...]
        o_ref[...] = _dot(act.astype(BF16), wdb_ref[...]).astype(o_ref.dtype)

    @pl.when(tv_ref[i] == 0)
    def _():
        o_ref[...] = jnp.zeros_like(o_ref)


def expert_ffn(x, tok_sorted, gate_sorted, tile_expert, tile_valid, w_gate, w_up, w_down, layer, tm):
    d = x.shape[1]
    p = tok_sorted.shape[0]
    f = w_gate.shape[-1]
    ntiles = p // tm
    grid_spec = pltpu.PrefetchScalarGridSpec(
        num_scalar_prefetch=3,
        grid=(ntiles,),
        in_specs=[pl.BlockSpec(memory_space=pl.ANY),
                  pl.BlockSpec((tm, 1), lambda i, te, tv, tok: (i, 0)),
                  pl.BlockSpec((1, 1, d, f), lambda i, te, tv, tok: (layer, te[i], 0, 0)),
                  pl.BlockSpec((1, 1, d, f), lambda i, te, tv, tok: (layer, te[i], 0, 0)),
                  pl.BlockSpec((1, 1, f, d), lambda i, te, tv, tok: (layer, te[i], 0, 0))],
        out_specs=pl.BlockSpec((tm, d), lambda i, te, tv, tok: (i, 0)),
        scratch_shapes=[pltpu.VMEM((d, f), BF16), pltpu.VMEM((d, f), BF16), pltpu.VMEM((f, d), BF16),
                        pltpu.VMEM((2, tm, d), F32), pltpu.SemaphoreType.DMA((2,))],
    )
    return pl.pallas_call(
        _expert_body,
        grid_spec=grid_spec,
        out_shape=jax.ShapeDtypeStruct((p, d), BF16),
        compiler_params=_cparams(("arbitrary",)),
        name="expert_ffn",
    )(tile_expert, tile_valid, tok_sorted, x, gate_sorted, w_gate, w_up, w_down)


MOE_TILE = 256


def hier_moe(x_f32, wg, bg, we, be, w_gate, w_up, w_down, layer):
    n, d = x_f32.shape
    tm = MOE_TILE
    eid, gates, rank, counts = router(x_f32, wg, bg, we, be)
    padded = ((counts + tm - 1) // tm) * tm
    seg_end = jnp.cumsum(padded)
    seg_start = seg_end - padded
    experts = jnp.arange(N_EXPERTS, dtype=jnp.int32)
    pos = (jnp.sum(jnp.where(eid[..., None] == experts, seg_start, 0), axis=-1) + rank).reshape(-1)
    p = 2 * n + N_EXPERTS * tm
    tok = jnp.arange(2 * n, dtype=jnp.int32) // 2
    packed = jnp.stack([tok, lax.bitcast_convert_type(gates.reshape(-1), jnp.int32)], axis=-1)
    packed = jnp.zeros((p, 2), jnp.int32).at[pos].set(packed)
    tok_sorted = packed[:, 0]
    gate_sorted = lax.bitcast_convert_type(packed[:, 1], F32)
    tile_start = jnp.arange(p // tm, dtype=jnp.int32) * tm
    tile_valid = (tile_start < seg_end[-1]).astype(jnp.int32)
    tile_expert = jnp.sum((tile_start[:, None] >= seg_end[None, :]).astype(jnp.int32), axis=1)
    tile_expert = jnp.minimum(tile_expert, N_EXPERTS - 1)
    last_e = jnp.max(jnp.where(tile_valid > 0, tile_expert, 0))
    tile_expert = jnp.where(tile_valid > 0, tile_expert, last_e).astype(jnp.int32)
    ys = expert_ffn(x_f32, tok_sorted, gate_sorted.reshape(p, 1), tile_expert, tile_valid,
                    w_gate, w_up, w_down, layer, tm)
    pos2 = pos.reshape(n, 2)
    return (ys.at[pos2[:, 0]].get(mode="promise_in_bounds"),
            ys.at[pos2[:, 1]].get(mode="promise_in_bounds"))


C_HEAD = 128
HGRN_LEVELS = (1, 2, 4, 8, 16, 32)


def _cumsum_rows(x, row):
    axis = x.ndim - 2
    s = 1
    while s < x.shape[axis]:
        x = x + jnp.where(row >= s, pltpu.roll(x, s, axis=axis), 0.0)
        s *= 2
    return x


def _anchor_rows(b, m):
    n = b.shape[0]
    if m >= 8:
        parts = [jnp.broadcast_to(b[base + m:base + m + 1, :], (2 * m, b.shape[1]))
                 for base in range(0, n, 2 * m)]
        return parts[0] if len(parts) == 1 else jnp.concatenate(parts, axis=0)
    b3 = b.reshape(n // 8, 8, b.shape[1])
    sub = lax.broadcasted_iota(jnp.int32, b3.shape, 1)
    out = None
    for base in range(8 - 2 * m, -1, -2 * m):
        mid = jnp.broadcast_to(b3[:, base + m:base + m + 1, :], b3.shape)
        out = mid if out is None else jnp.where(sub < base + 2 * m, mid, out)
    return out.reshape(n, b.shape[1])


def _hgrn2_body(q_ref, f_ref, i_ref, g_ref, loglb_ref, log1mlb_ref, omlb_ref, gain_ref,
                o_ref, st_ref, *, hb, n_chunk):
    @pl.when(pl.program_id(2) == 0)
    def _():
        st_ref[...] = jnp.zeros_like(st_ref)

    row = lax.broadcasted_iota(jnp.int32, (CHUNK, hb * C_HEAD), 0)
    tt = lax.broadcasted_iota(jnp.int32, (CHUNK, CHUNK), 0)
    ss = lax.broadcasted_iota(jnp.int32, (CHUNK, CHUNK), 1)
    txs = tt ^ ss
    lvl_id = jnp.where(tt == ss, 0, -1)
    for li, m in enumerate(HGRN_LEVELS):
        lvl_id = jnp.where((ss < tt) & (txs >= m) & (txs < 2 * m), li + 1, lvl_id)

    heads = [slice(h * C_HEAD, (h + 1) * C_HEAD) for h in range(hb)]

    def head_scores(qm, km):
        return jnp.stack([_dot_nt(qm[:, hs], km[:, hs]) for hs in heads])

    def chunk_step(c, carry):
        rs = pl.ds(pl.multiple_of(c * CHUNK, CHUNK), CHUNK)
        q = q_ref[rs, :]
        f = f_ref[rs, :]
        v = i_ref[rs, :].astype(BF16)
        g = g_ref[rs, :]
        qs = q * jax.nn.sigmoid(q)
        log_sig = jnp.minimum(f, 0.0) - jnp.log1p(jnp.exp(-jnp.abs(f)))
        y = log1mlb_ref[...] + log_sig
        x = loglb_ref[...]
        log_f = jnp.maximum(x, y) + jnp.log1p(jnp.exp(-jnp.abs(x - y)))
        key = omlb_ref[...] * jax.nn.sigmoid(-f)
        b = _cumsum_rows(log_f, row)
        scores = jnp.where(lvl_id == 0, head_scores(qs.astype(BF16), key.astype(BF16)), 0.0)
        for li, m in enumerate(HGRN_LEVELS):
            upper = (row & m) != 0
            e = jnp.exp(jnp.where(upper, 1.0, -1.0) * (b - _anchor_rows(b, m)))
            xm = (jnp.where(upper, qs, key) * e).astype(BF16)
            scores = jnp.where(lvl_id == li + 1, head_scores(xm, xm), scores)
        scores = scores.astype(BF16)
        q_dec = (qs * jnp.exp(b)).astype(BF16)
        b_last = b[CHUNK - 1:CHUNK, :]
        k_dec = (key * jnp.exp(b_last - b)).astype(BF16)
        s_dec = jnp.exp(b_last)
        outs = []
        for h, hs in enumerate(heads):
            st = st_ref[h]
            o = _dot(scores[h], v[:, hs]) + _dot_nt(q_dec[:, hs], st.astype(BF16))
            st_ref[h] = st * s_dec[:, hs] + _dot_tn(v[:, hs], k_dec[:, hs])
            outs.append(o * lax.rsqrt(jnp.mean(o * o, axis=-1, keepdims=True) + RMS_EPS))
        o = jnp.concatenate(outs, axis=1) * gain_ref[...] * (g * jax.nn.sigmoid(g))
        o_ref[rs, :] = o.astype(o_ref.dtype)
        return carry

    lax.fori_loop(0, n_chunk, chunk_step, 0)


def hgrn2(cols, lb, norm_gain, bsz, seq, hb=8, tc=512):
    n, c4 = cols.shape
    c = c4 // 4
    wl = hb * C_HEAD
    nhb = c // wl
    nt = seq // tc
    lb = lb.astype(F32).reshape(1, c)
    loglb = jnp.log(lb)
    log1mlb = jnp.log1p(-lb)
    omlb = 1.0 - lb

    def col(k):
        return pl.BlockSpec((tc, wl), lambda b, h, t, k=k: (b * nt + t, k * nhb + h))

    vec = pl.BlockSpec((1, wl), lambda b, h, t: (0, h))
    return pl.pallas_call(
        functools.partial(_hgrn2_body, hb=hb, n_chunk=tc // CHUNK),
        grid=(bsz, nhb, nt),
        in_specs=[col(0), col(1), col(2), col(3), vec, vec, vec, vec],
        out_specs=pl.BlockSpec((tc, wl), lambda b, h, t: (b * nt + t, h)),
        out_shape=jax.ShapeDtypeStruct((n, c), BF16),
        scratch_shapes=[pltpu.VMEM((hb, C_HEAD, C_HEAD), F32)],
        compiler_params=_cparams(("parallel", "parallel", "arbitrary")),
        name="hgrn2",
    )(cols, cols, cols, cols, loglb, log1mlb, omlb, norm_gain.reshape(1, c))


B_WIDTH = 2048
B_HEAD = 64
B_HEADS = B_WIDTH // B_HEAD
B_LORA_PAD = 128
B_GATE_LORA = 256
B_COLS_PAD = 3 * B_WIDTH + 2 * B_LORA_PAD + B_GATE_LORA


def _head_pair_ones():
    r = lax.broadcasted_iota(jnp.int32, (LANES, LANES), 0) // B_HEAD
    c = lax.broadcasted_iota(jnp.int32, (LANES, LANES), 1) // B_HEAD
    return (r == c).astype(BF16)


def _head_sums(x, ones):
    outs = []
    for j in range(x.shape[1] // LANES):
        hi, lo = _split_bf16(x[:, j * LANES:(j + 1) * LANES])
        outs.append(_dot(hi, ones) + _dot(lo, ones))
    return outs[0] if len(outs) == 1 else jnp.concatenate(outs, axis=1)


def _dot3(a, wh, wl):
    ah, al = _split_bf16(a)
    return _dot(ah, wh) + (_dot(al, wh) + _dot(ah, wl))


def _softplus(z):
    return jnp.maximum(z, 0.0) + jnp.log1p(jnp.exp(-jnp.abs(z)))


def _rwkv_prep_body(c_ref, p_ref, mu_ref, w0_ref, a0_ref, kk_ref, ka_ref,
                    wuh_ref, wul_ref, auh_ref, aul_ref, gu_ref,
                    r_o, lw_o, k_o, v_o, kk_o, a_o, g_o, *, tiles_per_seq):
    i = pl.program_id(0)
    x = c_ref[...]
    tm = x.shape[0]
    row = lax.broadcasted_iota(jnp.int32, x.shape, 0)
    prev = jnp.where(i % tiles_per_seq == 0, 0.0, p_ref[7:8, :])
    shifted = jnp.where(row == 0, prev, pltpu.roll(x, 1, axis=0))
    mixed = x + (shifted - x) * mu_ref[...]
    w = B_WIDTH
    r = mixed[:, 0:w]
    k = mixed[:, w:2 * w]
    v = mixed[:, 2 * w:3 * w]
    wd = mixed[:, 3 * w:3 * w + B_LORA_PAD]
    ad = mixed[:, 3 * w + B_LORA_PAD:3 * w + 2 * B_LORA_PAD]
    gd = mixed[:, 3 * w + 2 * B_LORA_PAD:]
    wlog = -_softplus(-(w0_ref[...] + _dot3(jnp.tanh(wd), wuh_ref[...], wul_ref[...]))) - 0.5
    a = jax.nn.sigmoid(a0_ref[...] + _dot3(ad, auh_ref[...], aul_ref[...]))
    g = _dot(jax.nn.sigmoid(gd).astype(BF16), gu_ref[...])
    kk = k * kk_ref[...]
    ssq = _head_sums(kk * kk, _head_pair_ones())
    kk = kk / jnp.maximum(jnp.sqrt(ssq), 1e-12)
    r_o[...] = r
    lw_o[...] = -jnp.exp(wlog)
    k_o[...] = k * (1.0 + (a - 1.0) * ka_ref[...])
    v_o[...] = v
    kk_o[...] = kk
    a_o[...] = a
    g_o[...] = g


def _pad_rows(w, rows):
    return jnp.zeros((rows, w.shape[1]), w.dtype).at[:w.shape[0]].set(w)


def rwkv_prep(cols, mu_pad, w0, w_up, a0, a_up, g_up, k_k, k_a, seq, tm=256):
    n, cw = cols.shape
    w = B_WIDTH
    wuh, wul = _split_bf16(_pad_rows(w_up, B_LORA_PAD))
    auh, aul = _split_bf16(_pad_rows(a_up, B_LORA_PAD))
    row = pl.BlockSpec((tm, w), lambda i: (i, 0))
    vec = pl.BlockSpec((1, w), lambda i: (0, 0))
    lora = pl.BlockSpec((B_LORA_PAD, w), lambda i: (0, 0))
    outs = pl.pallas_call(
        functools.partial(_rwkv_prep_body, tiles_per_seq=seq // tm),
        grid=(n // tm,),
        in_specs=[pl.BlockSpec((tm, cw), lambda i: (i, 0)),
                  pl.BlockSpec((8, cw), lambda i: (jnp.maximum(i * (tm // 8) - 1, 0), 0)),
                  pl.BlockSpec((1, cw), lambda i: (0, 0)),
                  vec, vec, vec, vec, lora, lora, lora, lora,
                  pl.BlockSpec((B_GATE_LORA, w), lambda i: (0, 0))],
        out_specs=[row] * 7,
        out_shape=[jax.ShapeDtypeStruct((n, w), F32)] * 7,
        compiler_params=_cparams(("parallel",)),
        name="rwkv_prep",
    )(cols, cols, mu_pad.reshape(1, cw), w0.reshape(1, w), a0.reshape(1, w), k_k.reshape(1, w),
      k_a.reshape(1, w), wuh, wul, auh, aul, g_up.astype(BF16))
    return outs


def _rwkv_scan_body(r_ref, lw_ref, k_ref, v_ref, kk_ref, a_ref, y_ref, st_ref, qm_ref, ys_ref, *, hb, n_chunk):
    @pl.when(pl.program_id(2) == 0)
    def _():
        st_ref[...] = jnp.zeros_like(st_ref)

    L = CHUNK
    row = lax.broadcasted_iota(jnp.int32, (hb, L, B_HEAD), 1)
    tt = lax.broadcasted_iota(jnp.int32, (L, L), 0)
    ss = lax.broadcasted_iota(jnp.int32, (L, L), 1)
    strict = ss < tt
    incl = ss <= tt
    eye = ss == tt
    eye_f = eye.astype(F32)

    def local_step(c, carry):
        sl = pl.ds(pl.multiple_of(c * L, L), L)
        r = r_ref[:, sl, :]
        lw = lw_ref[:, sl, :]
        k = k_ref[:, sl, :]
        v = v_ref[:, sl, :]
        kk = kk_ref[:, sl, :]
        a = a_ref[:, sl, :]
        lp = _cumsum_rows(lw, row)
        p = jnp.exp(lp)
        inv_p = jnp.exp(-lp)
        at = -kk * jnp.exp(lp - lw)
        bt = kk * a * inv_p
        kt = k * inv_p
        rt = r * p
        p_last = p[:, L - 1:L, :]
        bp = bt * p_last
        kp = kt * p_last
        ar = jnp.concatenate([at, rt], axis=1)
        m_b = _bdot_nt(ar, bt)
        m_k = _bdot_nt(ar, kt)
        nmat = jnp.where(strict, m_b[:, :L], 0.0)
        a_rb = jnp.where(incl, m_b[:, L:], 0.0)
        g = _bdot(jnp.where(strict, m_k[:, :L], 0.0), v)
        tmat = eye_f + nmat
        npow = nmat
        lvl = 2
        while lvl < L:
            npow = _bdot(npow, npow)
            tmat = tmat + _bdot(npow, tmat)
            lvl *= 2
        w = _bdot(tmat, at)
        u = _bdot(tmat, g)
        qe = rt + _bdot(a_rb, w)
        yl = _bdot(a_rb, u) + _bdot(jnp.where(incl, m_k[:, L:], 0.0), v)
        qm_ref[:, c, 0:L, :] = qe
        ys_ref[:, c, 0:L, :] = yl
        for h in range(hb):
            qm_ref[h, c, L:2 * L, :] = _dot_tn(bp[h], w[h]) + jnp.where(eye, p_last[h], 0.0)
            ys_ref[h, c, L:2 * L, :] = _dot_tn(bp[h], u[h]) + _dot_tn(kp[h], v[h])
        return carry

    lax.fori_loop(0, n_chunk, local_step, 0)

    def serial_step(c, carry):
        sl = pl.ds(pl.multiple_of(c * L, L), L)
        out = _bdot(qm_ref[:, c], st_ref[...]) + ys_ref[:, c]
        y_ref[:, sl, :] = out[:, :L]
        st_ref[...] = out[:, L:]
        return carry

    lax.fori_loop(0, n_chunk, serial_step, 0)


def rwkv_scan(r, lw, k, v, kk, a, bsz, seq, hb=16, tc=256):
    nh, n, hd = r.shape
    tc = min(tc, seq)
    nt = seq // tc
    n_chunk = tc // CHUNK
    blk = pl.BlockSpec((hb, tc, hd), lambda b, h, t: (h, b * nt + t, 0))
    return pl.pallas_call(
        functools.partial(_rwkv_scan_body, hb=hb, n_chunk=n_chunk),
        grid=(bsz, nh // hb, nt),
        in_specs=[blk] * 6,
        out_specs=blk,
        out_shape=jax.ShapeDtypeStruct((nh, n, hd), F32),
        scratch_shapes=[pltpu.VMEM((hb, hd, hd), F32),
                        pltpu.VMEM((hb, n_chunk, 2 * CHUNK, hd), F32),
                        pltpu.VMEM((hb, n_chunk, 2 * CHUNK, hd), F32)],
        compiler_params=_cparams(("parallel", "parallel", "arbitrary")),
        name="rwkv_scan",
    )(r, lw, k, v, kk, a)


def _rwkv_post_body(y_ref, r_ref, k_ref, v_ref, g_ref, rk_ref, lg_ref, lb_ref, o_ref):
    ones = _head_pair_ones()
    y = y_ref[...]
    mean = _head_sums(y, ones) * (1.0 / B_HEAD)
    yc = y - mean
    var = _head_sums(yc * yc, ones) * (1.0 / B_HEAD)
    yn = yc * lax.rsqrt(var + GN_EPS) * lg_ref[...] + lb_ref[...]
    v = v_ref[...]
    bonus = _head_sums(r_ref[...] * k_ref[...] * rk_ref[...], ones) * v
    o_ref[...] = ((yn + bonus) * g_ref[...]).astype(o_ref.dtype)


def rwkv_post(y, r, k, v, g, r_k, ln_gain, ln_bias, tm=256):
    n, w = y.shape
    row = pl.BlockSpec((tm, w), lambda i: (i, 0))
    vec = pl.BlockSpec((1, w), lambda i: (0, 0))
    return pl.pallas_call(
        _rwkv_post_body,
        grid=(n // tm,),
        in_specs=[row] * 5 + [vec] * 3,
        out_specs=row,
        out_shape=jax.ShapeDtypeStruct((n, w), BF16),
        compiler_params=_cparams(("parallel",)),
        name="rwkv_post",
    )(y, r, k, v, g, r_k.reshape(1, w), ln_gain.reshape(1, w), ln_bias.reshape(1, w))


def _to_heads(z):
    n = z.shape[0]
    return z.reshape(n, B_HEADS, B_HEAD).transpose(1, 0, 2)


def mixer_rwkv7(cols_b, mu_pad, w0, w_up, a0, a_up, g_up, k_k, k_a, r_k, ln_gain, ln_bias, bsz, seq):
    r, lw, k, v, kk, a, g = rwkv_prep(cols_b, mu_pad, w0, w_up, a0, a_up, g_up, k_k, k_a, seq)
    yh = rwkv_scan(*[_to_heads(z) for z in (r, lw, k, v, kk, a)], bsz, seq)
    y = yh.transpose(1, 0, 2).reshape(r.shape)
    return rwkv_post(y, r, k, v, g, r_k.reshape(-1), ln_gain, ln_bias)


A_Q_RANK = 768
A_KV_RANK = 512
IDX_DIM = 64
IDX_HEADS = 16
A_HEADS = 16
A_HEAD_DIM = 128
A_COLS_PAD = A_Q_RANK + A_KV_RANK + 2 * LANES
INT_MIN = -2 ** 31


def _dsa_prep_body(c_ref, qn_ref, kn_ref, g_ref, b_ref, q_o, c_o, k_o, w_o):
    x = c_ref[...]
    ql = x[:, :A_Q_RANK]
    q_o[...] = (ql * lax.rsqrt(jnp.mean(ql * ql, axis=-1, keepdims=True) + RMS_EPS) * qn_ref[...]).astype(q_o.dtype)
    kv = x[:, A_Q_RANK:A_Q_RANK + A_KV_RANK]
    c_o[...] = (kv * lax.rsqrt(jnp.mean(kv * kv, axis=-1, keepdims=True) + RMS_EPS) * kn_ref[...]).astype(c_o.dtype)
    o = A_Q_RANK + A_KV_RANK
    ki = x[:, o:o + LANES]
    valid = lax.broadcasted_iota(jnp.int32, ki.shape, 1) < IDX_DIM
    mu = jnp.sum(jnp.where(valid, ki, 0.0), axis=-1, keepdims=True) * (1.0 / IDX_DIM)
    kc = jnp.where(valid, ki - mu, 0.0)
    var = jnp.sum(kc * kc, axis=-1, keepdims=True) * (1.0 / IDX_DIM)
    kn = kc * lax.rsqrt(var + LN_EPS) * g_ref[...] + b_ref[...]
    k_o[...] = kn[:, :IDX_DIM].astype(k_o.dtype)
    w_o[...] = x[:, o + LANES:o + 2 * LANES] * (IDX_HEADS ** -0.5 * IDX_DIM ** -0.5)


def dsa_prep(cols, q_norm, kv_norm, k_gain, k_bias, tm=512):
    n, cw = cols.shape
    pad = jnp.zeros((LANES - IDX_DIM,), F32)
    return pl.pallas_call(
        _dsa_prep_body,
        grid=(n // tm,),
        in_specs=[pl.BlockSpec((tm, cw), lambda i: (i, 0)),
                  pl.BlockSpec((1, A_Q_RANK), lambda i: (0, 0)),
                  pl.BlockSpec((1, A_KV_RANK), lambda i: (0, 0)),
                  pl.BlockSpec((1, LANES), lambda i: (0, 0)),
                  pl.BlockSpec((1, LANES), lambda i: (0, 0))],
        out_specs=[pl.BlockSpec((tm, A_Q_RANK), lambda i: (i, 0)),
                   pl.BlockSpec((tm, A_KV_RANK), lambda i: (i, 0)),
                   pl.BlockSpec((tm, IDX_DIM), lambda i: (i, 0)),
                   pl.BlockSpec((tm, LANES), lambda i: (i, 0))],
        out_shape=[jax.ShapeDtypeStruct((n, A_Q_RANK), BF16), jax.ShapeDtypeStruct((n, A_KV_RANK), BF16),
                   jax.ShapeDtypeStruct((n, IDX_DIM), BF16), jax.ShapeDtypeStruct((n, LANES), F32)],
        compiler_params=_cparams(("parallel",)),
        name="dsa_prep",
    )(cols, q_norm.reshape(1, -1), kv_norm.reshape(1, -1),
      jnp.concatenate([k_gain, pad]).reshape(1, LANES), jnp.concatenate([k_bias, pad]).reshape(1, LANES))


def _head_mm_body(a_ref, w_ref, o_ref, *, scale):
    o_ref[0] = (_dot(a_ref[...], w_ref[0]) * scale).astype(o_ref.dtype)


def head_matmul(a, w, scale, tm=1024):
    n = a.shape[0]
    nh, k, f = w.shape
    tm = min(tm, n)
    return pl.pallas_call(
        functools.partial(_head_mm_body, scale=scale),
        grid=(nh, n // tm),
        in_specs=[pl.BlockSpec((tm, k), lambda h, i: (i, h)),
                  pl.BlockSpec((1, k, f), lambda h, i: (h, 0, 0))],
        out_specs=pl.BlockSpec((1, tm, f), lambda h, i: (h, i, 0)),
        out_shape=jax.ShapeDtypeStruct((nh, n, f), BF16),
        compiler_params=_cparams(("parallel", "parallel")),
        name="head_matmul",
    )(a, w)


DSA_KT = 256
DSA_AT = 512
DSA_HEAD_GROUP = 4


def _dsa_attend_body(qi_ref, wi_ref, qa_ref, ck_ref, kx_ref, wuv_ref, sl_ref, o_ref,
                     keys_ref, m_ref, l_ref, acc_ref, *, top_k, seq):
    j = pl.program_id(1)
    at_n = ((j + 1) * Q_BLOCK + DSA_AT - 1) // DSA_AT
    kt_n = at_n * (DSA_AT // DSA_KT)
    nq = Q_BLOCK
    nh = A_HEADS
    qrow = lax.broadcasted_iota(jnp.int32, (nq, 1), 0)
    qpos = j * Q_BLOCK + qrow
    limit = (qpos // CHUNK + 1) * CHUNK
    lane = lax.broadcasted_iota(jnp.int32, (nq, DSA_KT), 1)
    lane_a = lax.broadcasted_iota(jnp.int32, (nq, DSA_AT), 1)

    qi = qi_ref[...].reshape(nh * nq, IDX_DIM)
    wi = wi_ref[...]

    def score_tile(kt, carry):
        starts = [pl.multiple_of(kt * DSA_AT + t * DSA_KT, DSA_KT) for t in range(DSA_AT // DSA_KT)]
        rels = [_dot_nt(qi, kx_ref[pl.ds(k0, DSA_KT), :]) for k0 in starts]
        for k0, rel in zip(starts, rels):
            score = jnp.sum(jnp.maximum(rel, 0.0).reshape(nh, nq, DSA_KT) * wi, axis=0)
            bits = lax.bitcast_convert_type(score, jnp.int32)
            key = bits ^ ((bits >> 31) & 0x7FFFFFFF)
            keys_ref[:, pl.ds(k0, DSA_KT)] = jnp.where(k0 + lane < limit, key, INT_MIN)
        return carry

    lax.fori_loop(0, at_n, score_tile, 0)

    def count_keys(pred):
        def count_tile(kt, cnt):
            k0 = pl.multiple_of(kt * DSA_AT, DSA_AT)
            hit = pred(keys_ref[:, pl.ds(k0, DSA_AT)], k0).astype(jnp.int32)
            for t in range(DSA_AT // LANES):
                cnt = cnt + hit[:, t * LANES:(t + 1) * LANES]
            return cnt

        cnt = lax.fori_loop(0, at_n, count_tile, jnp.zeros((nq, LANES), jnp.int32))
        return jnp.sum(cnt, axis=-1, keepdims=True)

    def bit_step(i, carry):
        ans, n_ge = carry
        cand = ans | lax.shift_left(jnp.int32(1), 31 - i)
        cand_s = cand ^ INT_MIN
        total = count_keys(lambda key, k0: key >= cand_s)
        ok = total >= top_k
        return jnp.where(ok, cand, ans), jnp.where(ok, total, n_ge)

    zero = jnp.zeros((nq, 1), jnp.int32)
    ans, n_ge = lax.fori_loop(0, 32, bit_step, (zero, zero))
    thr = ans ^ INT_MIN

    excess = (ans != 0) & (n_ge > top_k)
    idx_bits = max(1, (seq - 1).bit_length())

    def tie_cut():
        need = top_k - count_keys(lambda key, k0: key > thr)

        def idx_step(i, cut):
            cand = cut | lax.shift_left(jnp.int32(1), idx_bits - 1 - i)
            below = count_keys(lambda key, k0: (key == thr) & (k0 + lane_a < cand))
            return jnp.where(below < need, cand, cut)

        return lax.fori_loop(0, idx_bits, idx_step, zero)

    cut = lax.cond(jnp.max(excess.astype(jnp.int32)) > 0, tie_cut, lambda: zero)
    cut = jnp.where(excess, cut, seq)

    m_ref[...] = jnp.full_like(m_ref, -jnp.inf)
    l_ref[...] = jnp.zeros_like(l_ref)
    acc_ref[...] = jnp.zeros_like(acc_ref)
    slopes = sl_ref[...]
    gh = DSA_HEAD_GROUP

    def attend_tile(kt, carry):
        k0 = pl.multiple_of(kt * DSA_AT, DSA_AT)
        ck = ck_ref[pl.ds(k0, DSA_AT), :]
        key = keys_ref[:, pl.ds(k0, DSA_AT)]
        kidx = k0 + lane_a
        keep = ((key > thr) | ((key == thr) & (kidx <= cut))) & (key > INT_MIN)
        mask_bias = jnp.where(keep, 0.0, -jnp.inf)
        neg_dist = -jnp.abs(qpos - kidx).astype(F32)
        def qk(g):
            return _dot_nt(qa_ref[g * gh:(g + 1) * gh].reshape(gh * nq, A_KV_RANK), ck)

        s_next = qk(0)
        for g in range(nh // gh):
            s = s_next
            if g + 1 < nh // gh:
                s_next = qk(g + 1)
            rows = pl.ds(g * gh * nq, gh * nq)
            s = s.reshape(gh, nq, DSA_AT)
            s = (s + (slopes[g * gh:(g + 1) * gh] * neg_dist + mask_bias)).reshape(gh * nq, DSA_AT)
            m_old = m_ref[rows, :]
            m_new = jnp.maximum(m_old, jnp.max(s, axis=-1, keepdims=True))
            m_safe = jnp.where(m_new == -jnp.inf, 0.0, m_new)
            p = jnp.exp(s - m_safe)
            alpha = jnp.exp(m_old - m_safe)
            l_ref[rows, :] = alpha * l_ref[rows, :] + jnp.sum(p, axis=-1, keepdims=True)
            acc_ref[rows, :] = alpha * acc_ref[rows, :] + _dot(p.astype(BF16), ck)
            m_ref[rows, :] = m_new
        return carry

    lax.fori_loop(0, at_n, attend_tile, 0)
    o_lat = (acc_ref[...] / l_ref[...]).astype(BF16).reshape(nh, nq, A_KV_RANK)
    for h in range(nh):
        o_ref[:, h * A_HEAD_DIM:(h + 1) * A_HEAD_DIM] = _dot(o_lat[h], wuv_ref[h]).astype(o_ref.dtype)


def dsa_attend(qi, wi, qa, ckv, kx, w_uv, bsz, seq):
    n = ckv.shape[0]
    nblk = seq // Q_BLOCK
    top_k = min(TOPK_MAX, seq // 4)
    start = 2.0 ** (-8.0 / A_HEADS)
    slopes = jnp.asarray([start ** (h + 1) for h in range(A_HEADS)], F32).reshape(A_HEADS, 1, 1)

    def qblk(last):
        return pl.BlockSpec((A_HEADS, Q_BLOCK, last), lambda b, j: (0, b * nblk + j, 0))

    return pl.pallas_call(
        functools.partial(_dsa_attend_body, top_k=top_k, seq=seq),
        grid=(bsz, nblk),
        in_specs=[qblk(IDX_DIM), qblk(1), qblk(A_KV_RANK),
                  pl.BlockSpec((seq, A_KV_RANK), lambda b, j: (b, 0)),
                  pl.BlockSpec((seq, IDX_DIM), lambda b, j: (b, 0)),
                  pl.BlockSpec((A_HEADS, A_KV_RANK, A_HEAD_DIM), lambda b, j: (0, 0, 0)),
                  pl.BlockSpec((A_HEADS, 1, 1), lambda b, j: (0, 0, 0))],
        out_specs=pl.BlockSpec((Q_BLOCK, A_HEADS * A_HEAD_DIM), lambda b, j: (b * nblk + j, 0)),
        out_shape=jax.ShapeDtypeStruct((n, A_HEADS * A_HEAD_DIM), BF16),
        scratch_shapes=[pltpu.VMEM((Q_BLOCK, seq), jnp.int32),
                        pltpu.VMEM((A_HEADS * Q_BLOCK, 1), F32),
                        pltpu.VMEM((A_HEADS * Q_BLOCK, 1), F32),
                        pltpu.VMEM((A_HEADS * Q_BLOCK, A_KV_RANK), F32)],
        compiler_params=_cparams(("parallel", "arbitrary")),
        name="dsa_attend",
    )(qi, wi, qa, ckv, kx, w_uv, slopes)


def mixer_dsa(cols_a, q_norm, kv_norm, w_uq, w_uk, w_uv, w_idx_q, k_gain, k_bias, bsz, seq):
    n = cols_a.shape[0]
    q_lat, c_kv, k_idx, w_idx = dsa_prep(cols_a, q_norm, kv_norm, k_gain, k_bias)
    q = matmul(q_lat, w_uq.reshape(A_Q_RANK, -1).astype(BF16), out_dtype=BF16)
    qa = head_matmul(q, w_uk.transpose(1, 2, 0).astype(BF16), A_HEAD_DIM ** -0.5)
    qi = matmul(q_lat, w_idx_q.reshape(A_Q_RANK, -1).astype(BF16), out_dtype=BF16)
    qi = qi.reshape(n, IDX_HEADS, IDX_DIM).transpose(1, 0, 2)
    wi = w_idx[:, :IDX_HEADS].T.reshape(IDX_HEADS, n, 1)
    return dsa_attend(qi, wi, qa, c_kv, k_idx, w_uv.transpose(1, 0, 2).astype(BF16), bsz, seq)


A_SPLITS = (A_Q_RANK, A_KV_RANK, IDX_DIM, IDX_HEADS)
A_COLS = sum(A_SPLITS)
DECAY_LORA = 96
AAA_LORA = 96


def _pad_cols(w, sizes, padded):
    out, o = [], 0
    for s, p in zip(sizes, padded):
        piece = w[..., o:o + s]
        if p > s:
            piece = jnp.concatenate([piece, jnp.zeros(w.shape[:-1] + (p - s,), w.dtype)], axis=-1)
        out.append(piece)
        o += s
    return jnp.concatenate(out, axis=-1)


def kernel(x, ln1_gain, ln1_bias, ln2_gain, ln2_bias, w_in_even, a_q_norm, a_kv_norm, a_w_uq, a_w_uk, a_w_uv, a_w_idx_q, a_idx_k_gain, a_idx_k_bias, b_mu, b_w0, b_w_up, b_a0, b_a_up, b_g_up, b_k_k, b_k_a, b_r_k, b_ln_gain, b_ln_bias, w_out_even, w_in_odd, c_lb_logits, c_norm_gain, w_out_odd, router_group, router_group_bias, router_expert, router_expert_bias, moe_w_gate, moe_w_up, moe_w_down):
    bsz, seq, d = x.shape
    n = bsz * seq
    lb_table = jnp.cumsum(jax.nn.softmax(c_lb_logits.astype(F32), axis=0), axis=0)
    lb_table = lb_table - lb_table[:1]
    a_sizes, a_padded = A_SPLITS, (A_Q_RANK, A_KV_RANK, LANES, LANES)
    b_sizes = (3 * B_WIDTH, DECAY_LORA, AAA_LORA, B_GATE_LORA)
    b_padded = (3 * B_WIDTH, B_LORA_PAD, B_LORA_PAD, B_GATE_LORA)
    xf = x.reshape(n, d)
    xb = xf.astype(BF16)
    for layer in range(DEPTH):
        j = layer // 2
        if layer % 2 == 0:
            w_in = w_in_even[j]
            w_a = _pad_cols(w_in[:, :A_COLS], a_sizes, a_padded).astype(BF16)
            w_b = _pad_cols(w_in[:, A_COLS:], b_sizes, b_padded).astype(BF16)
            cols_a = matmul(xb, w_a, tn=512)
            cols_b = matmul(xb, w_b, tn=512)
            y_a = mixer_dsa(cols_a, a_q_norm[j], a_kv_norm[j], a_w_uq[j], a_w_uk[j], a_w_uv[j],
                            a_w_idx_q[j], a_idx_k_gain[j], a_idx_k_bias[j], bsz, seq)
            y_b = mixer_rwkv7(cols_b, _pad_cols(b_mu[j], b_sizes, b_padded), b_w0[j], b_w_up[j], b_a0[j],
                              b_a_up[j], b_g_up[j], b_k_k[j], b_k_a[j], b_r_k[j], b_ln_gain[j], b_ln_bias[j],
                              bsz, seq)
            mixed = jnp.concatenate([y_a, y_b], axis=-1)
            w_out = w_out_even[j]
        else:
            cols = matmul_stacked_w(xb, w_in_odd, j)
            mixed = hgrn2(cols, lb_table[j], c_norm_gain[j], bsz, seq)
            w_out = w_out_odd[j]
        xf, xb = proj_residual_ln(mixed, w_out.astype(BF16), xf, ln1_gain[layer], ln1_bias[layer])
        y0, y1 = hier_moe(xf, router_group[layer], router_group_bias[layer], router_expert[layer],
                          router_expert_bias[layer], moe_w_gate, moe_w_up, moe_w_down, layer)
        xf, xb = add_ln(xf, y0, y1, ln2_gain[layer], ln2_bias[layer])
    return xf.reshape(bsz, seq, d)
```

```python
import functools

import jax
import jax.numpy as jnp
import numpy as np
from jax import lax
from jax.experimental import pallas as pl
from jax.experimental.pallas import tpu as pltpu

F32 = jnp.float32
BF16 = jnp.bfloat16

DEPTH = 4
ALPHA = (2 * DEPTH) ** 0.25
LN_EPS = 1e-5
RMS_EPS = 1e-6
GN_EPS = 64e-5
CHUNK = 64
Q_BLOCK = 128
TOPK_MAX = 256
N_GROUPS = 4
EXPERTS_PER_GROUP = 8
N_EXPERTS = N_GROUPS * EXPERTS_PER_GROUP

LANES = 128
VMEM_LIMIT = 56 * 1024 * 1024


def _cparams(sem):
    return pltpu.CompilerParams(dimension_semantics=sem, vmem_limit_bytes=VMEM_LIMIT)


def _dot(a, b):
    return jnp.dot(a, b, preferred_element_type=F32)


def _dot_nt(a, b):
    return lax.dot_general(a, b, (((1,), (1,)), ((), ())), preferred_element_type=F32)


def _dot_tn(a, b):
    return lax.dot_general(a, b, (((0,), (0,)), ((), ())), preferred_element_type=F32)


def _bdot(a, b):
    return lax.dot_general(a, b, (((2,), (1,)), ((0,), (0,))), preferred_element_type=F32)


def _bdot_nt(a, b):
    return lax.dot_general(a, b, (((2,), (2,)), ((0,), (0,))), preferred_element_type=F32)


def _split_bf16(x):
    hi = x.astype(BF16)
    lo = (x - hi.astype(F32)).astype(BF16)
    return hi, lo


def _mm_body(a_ref, b_ref, o_ref):
    o_ref[...] = _dot(a_ref[...].astype(BF16), b_ref[...]).astype(o_ref.dtype)


def matmul(a, b, out_dtype=F32, tm=512, tn=1024):
    m, k = a.shape
    _, n = b.shape
    tm = min(tm, m)
    tn = min(tn, n)
    assert m % tm == 0 and n % tn == 0
    return pl.pallas_call(
        _mm_body,
        grid=(n // tn, m // tm),
        in_specs=[pl.BlockSpec((tm, k), lambda j, i: (i, 0)),
                  pl.BlockSpec((k, tn), lambda j, i: (0, j))],
        out_specs=pl.BlockSpec((tm, tn), lambda j, i: (i, j)),
        out_shape=jax.ShapeDtypeStruct((m, n), out_dtype),
        compiler_params=_cparams(("parallel", "parallel")),
        name="matmul",
    )(a, b)


def _mm_wcast_body(a_ref, w_ref, o_ref, wb_ref):
    @pl.when(pl.program_id(1) == 0)
    def _():
        wb_ref[...] = w_ref[0].astype(BF16)

    o_ref[...] = _dot(a_ref[...], wb_ref[...]).astype(o_ref.dtype)


def matmul_stacked_w(a, w, layer, out_dtype=F32, tm=512, tn=1024):
    m, k = a.shape
    n = w.shape[2]
    tm = min(tm, m)
    return pl.pallas_call(
        _mm_wcast_body,
        grid=(n // tn, m // tm),
        in_specs=[pl.BlockSpec((tm, k), lambda j, i: (i, 0)),
                  pl.BlockSpec((1, k, tn), lambda j, i: (layer, 0, j), pipeline_mode=pl.Buffered(1))],
        out_specs=pl.BlockSpec((tm, tn), lambda j, i: (i, j)),
        out_shape=jax.ShapeDtypeStruct((m, n), out_dtype),
        scratch_shapes=[pltpu.VMEM((k, tn), BF16)],
        compiler_params=_cparams(("parallel", "arbitrary")),
        name="matmul_stacked_w",
    )(a, w)


def _ln_rows(z, gain, bias):
    mu = jnp.mean(z, axis=-1, keepdims=True)
    zc = z - mu
    var = jnp.mean(zc * zc, axis=-1, keepdims=True)
    return zc * lax.rsqrt(var + LN_EPS) * gain + bias


def _proj_ln_body(a_ref, w_ref, x_ref, g_ref, b_ref, o_ref, ob_ref, *, nk):
    kk = pl.program_id(1)

    @pl.when(kk == 0)
    def _():
        o_ref[...] = jnp.zeros_like(o_ref)

    o_ref[...] += _dot(a_ref[...].astype(BF16), w_ref[...])

    @pl.when(kk == nk - 1)
    def _():
        y = _ln_rows(ALPHA * x_ref[...] + o_ref[...], g_ref[...], b_ref[...])
        o_ref[...] = y
        ob_ref[...] = y.astype(BF16)


def proj_residual_ln(a, w, x, gain, bias, tm=512, tk=512):
    m, k = a.shape
    d = w.shape[1]
    tm = min(tm, m)
    nk = k // tk
    assert nk >= 2
    return pl.pallas_call(
        functools.partial(_proj_ln_body, nk=nk),
        grid=(m // tm, nk),
        in_specs=[pl.BlockSpec((tm, tk), lambda i, kk: (i, kk)),
                  pl.BlockSpec((tk, d), lambda i, kk: (kk, 0)),
                  pl.BlockSpec((tm, d), lambda i, kk: (i, 0), pipeline_mode=pl.Buffered(1)),
                  pl.BlockSpec((1, d), lambda i, kk: (0, 0)),
                  pl.BlockSpec((1, d), lambda i, kk: (0, 0))],
        out_specs=[pl.BlockSpec((tm, d), lambda i, kk: (i, 0)),
                   pl.BlockSpec((tm, d), lambda i, kk: (i, 0), pipeline_mode=pl.Buffered(1))],
        out_shape=[jax.ShapeDtypeStruct((m, d), F32), jax.ShapeDtypeStruct((m, d), BF16)],
        compiler_params=_cparams(("parallel", "arbitrary")),
        name="proj_residual_ln",
    )(a, w, x, gain.reshape(1, d), bias.reshape(1, d))


def _add_ln_body(x_ref, y0_ref, y1_ref, g_ref, b_ref, o_ref, ob_ref):
    z = ALPHA * x_ref[...] + (y0_ref[...].astype(F32) + y1_ref[...].astype(F32))
    y = _ln_rows(z, g_ref[...], b_ref[...])
    o_ref[...] = y
    ob_ref[...] = y.astype(BF16)


def add_ln(x, y0, y1, gain, bias, tm=256):
    m, d = x.shape
    row = pl.BlockSpec((tm, d), lambda i: (i, 0))
    vec = pl.BlockSpec((1, d), lambda i: (0, 0))
    return pl.pallas_call(
        _add_ln_body,
        grid=(m // tm,),
        in_specs=[row, row, row, vec, vec],
        out_specs=[row, row],
        out_shape=[jax.ShapeDtypeStruct((m, d), F32), jax.ShapeDtypeStruct((m, d), BF16)],
        compiler_params=_cparams(("parallel",)),
        name="add_ln",
    )(x, y0, y1, gain.reshape(1, d), bias.reshape(1, d))


def _router_body(x_ref, wh_ref, wl_ref, b_ref, tri_ref, eid_ref, gate_ref, rank_ref, cnt_ref, base_ref):
    xh, xl = _split_bf16(x_ref[...])
    wh = wh_ref[...]
    logits = _dot(xh, wh) + (_dot(xl, wh) + _dot(xh, wl_ref[...])) + b_ref[...]
    lane = lax.broadcasted_iota(jnp.int32, logits.shape, 1)
    neg = jnp.float32(-jnp.inf)
    is_g = lane < N_GROUPS
    gl = jnp.where(is_g, logits, neg)
    gmax = jnp.max(gl, axis=-1, keepdims=True)
    g_idx = jnp.min(jnp.where(gl == gmax, lane, LANES), axis=-1, keepdims=True)
    g_p = 1.0 / jnp.sum(jnp.exp(gl - gmax), axis=-1, keepdims=True)
    e_lane = lane - N_GROUPS
    in_grp = (e_lane >= g_idx * EXPERTS_PER_GROUP) & (e_lane < (g_idx + 1) * EXPERTS_PER_GROUP)
    el = jnp.where(in_grp, logits, neg)
    emax = jnp.max(el, axis=-1, keepdims=True)
    pe = jnp.exp(el - emax)
    pe = pe / jnp.sum(pe, axis=-1, keepdims=True)
    v1 = jnp.max(pe, axis=-1, keepdims=True)
    i1 = jnp.min(jnp.where(in_grp & (pe == v1), lane, LANES), axis=-1, keepdims=True)
    rest = in_grp & (lane != i1)
    pe2 = jnp.where(rest, pe, -1.0)
    v2 = jnp.max(pe2, axis=-1, keepdims=True)
    i2 = jnp.min(jnp.where(rest & (pe2 == v2), lane, LANES), axis=-1, keepdims=True)
    den = v1 + v2
    eid_ref[...] = jnp.where(lane == 0, i1 - N_GROUPS, i2 - N_GROUPS)
    gate_ref[...] = jnp.where(lane == 0, g_p * (v1 / den), g_p * (v2 / den))

    @pl.when(pl.program_id(0) == 0)
    def _():
        base_ref[...] = jnp.zeros_like(base_ref)

    hit1 = lane == i1
    hit2 = lane == i2
    oh1 = hit1.astype(BF16)
    oh2 = hit2.astype(BF16)
    tri = tri_ref[...]
    cnt1 = jnp.sum(oh1.astype(F32), axis=0, keepdims=True)
    cnt2 = jnp.sum(oh2.astype(F32), axis=0, keepdims=True)
    base = base_ref[...]
    r1 = jnp.sum(jnp.where(hit1, base + _dot(tri, oh1), 0.0), axis=-1, keepdims=True)
    r2 = jnp.sum(jnp.where(hit2, base + cnt1 + _dot(tri, oh2), 0.0), axis=-1, keepdims=True)
    rank_ref[...] = jnp.where(lane == 0, r1, r2).astype(jnp.int32)
    base_ref[...] = base + cnt1 + cnt2
    cnt_ref[...] = (base + cnt1 + cnt2).astype(jnp.int32)


def router(x, wg, bg, we, be, tm=512):
    m, d = x.shape
    tm = min(tm, m)
    w = jnp.zeros((d, LANES), F32).at[:, :N_GROUPS].set(wg).at[:, N_GROUPS:N_GROUPS + N_EXPERTS].set(we)
    b = jnp.zeros((1, LANES), F32).at[0, :N_GROUPS].set(bg).at[0, N_GROUPS:N_GROUPS + N_EXPERTS].set(be)
    wh, wl = _split_bf16(w)
    tri = jnp.tril(jnp.ones((tm, tm), BF16), k=-1)
    row = pl.BlockSpec((tm, LANES), lambda i: (i, 0))
    wspec = pl.BlockSpec((d, LANES), lambda i: (0, 0))
    vec = pl.BlockSpec((1, LANES), lambda i: (0, 0))
    eid, gate, rank, cnt = pl.pallas_call(
        _router_body,
        grid=(m // tm,),
        in_specs=[pl.BlockSpec((tm, d), lambda i: (i, 0)), wspec, wspec, vec,
                  pl.BlockSpec((tm, tm), lambda i: (0, 0))],
        out_specs=[row, row, row, vec],
        out_shape=[jax.ShapeDtypeStruct((m, LANES), jnp.int32), jax.ShapeDtypeStruct((m, LANES), F32),
                   jax.ShapeDtypeStruct((m, LANES), jnp.int32), jax.ShapeDtypeStruct((1, LANES), jnp.int32)],
        scratch_shapes=[pltpu.VMEM((1, LANES), F32)],
        compiler_params=_cparams(("arbitrary",)),
        name="router",
    )(x, wh, wl, b, tri)
    return eid[:, :2], gate[:, :2], rank[:, :2], cnt[0, N_GROUPS:N_GROUPS + N_EXPERTS]


def _expert_body(te_ref, tv_ref, tok_ref, x_hbm, gate_ref, wg_ref, wu_ref, wd_ref, o_ref,
                 wgb_ref, wub_ref, wdb_ref, xbuf_ref, sem_ref):
    i = pl.program_id(0)
    n_tiles = pl.num_programs(0)
    tm = xbuf_ref.shape[1]

    def start_gather(tile, slot):
        def issue(r, carry):
            tok = tok_ref[tile * tm + r]
            pltpu.make_async_copy(x_hbm.at[pl.ds(tok, 1)], xbuf_ref.at[slot, pl.ds(r, 1)],
                                  sem_ref.at[slot]).start()
            return carry

        lax.fori_loop(0, tm, issue, 0, unroll=8)

    @pl.when((i == 0) & (tv_ref[0] > 0))
    def _():
        start_gather(0, 0)

    nxt = jnp.minimum(i + 1, n_tiles - 1)

    @pl.when((i + 1 < n_tiles) & (tv_ref[nxt] > 0))
    def _():
        start_gather(i + 1, (i + 1) % 2)

    @pl.when((i == 0) | (te_ref[i] != te_ref[jnp.maximum(i - 1, 0)]))
    def _():
        wgb_ref[...] = wg_ref[0, 0].astype(BF16)
        wub_ref[...] = wu_ref[0, 0].astype(BF16)
        wdb_ref[...] = wd_ref[0, 0].astype(BF16)

    @pl.when(tv_ref[i] > 0)
    def _():
        slot = i % 2
        pltpu.make_async_copy(x_hbm.at[pl.ds(0, tm)], xbuf_ref.at[slot], sem_ref.at[slot]).wait()
        xs = xbuf_ref[slot].astype(BF16)
        hg = _dot(xs, wgb_ref[...])
        hu = _dot(xs, wub_ref[...])
        act = (hg * jax.nn.sigmoid(hg)) * hu * gate_ref[...]
        o_ref[...] = _dot(act.astype(BF16), wdb_ref[...]).astype(o_ref.dtype)

    @pl.when(tv_ref[i] == 0)
    def _():
        o_ref[...] = jnp.zeros_like(o_ref)


def expert_ffn(x, tok_sorted, gate_sorted, tile_expert, tile_valid, w_gate, w_up, w_down, layer, tm):
    d = x.shape[1]
    p = tok_sorted.shape[0]
    f = w_gate.shape[-1]
    ntiles = p // tm
    grid_spec = pltpu.PrefetchScalarGridSpec(
        num_scalar_prefetch=3,
        grid=(ntiles,),
        in_specs=[pl.BlockSpec(memory_space=pl.ANY),
                  pl.BlockSpec((tm, 1), lambda i, te, tv, tok: (i, 0)),
                  pl.BlockSpec((1, 1, d, f), lambda i, te, tv, tok: (layer, te[i], 0, 0)),
                  pl.BlockSpec((1, 1, d, f), lambda i, te, tv, tok: (layer, te[i], 0, 0)),
                  pl.BlockSpec((1, 1, f, d), lambda i, te, tv, tok: (layer, te[i], 0, 0))],
        out_specs=pl.BlockSpec((tm, d), lambda i, te, tv, tok: (i, 0)),
        scratch_shapes=[pltpu.VMEM((d, f), BF16), pltpu.VMEM((d, f), BF16), pltpu.VMEM((f, d), BF16),
                        pltpu.VMEM((2, tm, d), F32), pltpu.SemaphoreType.DMA((2,))],
    )
    return pl.pallas_call(
        _expert_body,
        grid_spec=grid_spec,
        out_shape=jax.ShapeDtypeStruct((p, d), BF16),
        compiler_params=_cparams(("arbitrary",)),
        name="expert_ffn",
    )(tile_expert, tile_valid, tok_sorted, x, gate_sorted, w_gate, w_up, w_down)


MOE_TILE = 256


def hier_moe(x_f32, wg, bg, we, be, w_gate, w_up, w_down, layer):
    n, d = x_f32.shape
    tm = MOE_TILE
    eid, gates, rank, counts = router(x_f32, wg, bg, we, be)
    padded = ((counts + tm - 1) // tm) * tm
    seg_end = jnp.cumsum(padded)
    seg_start = seg_end - padded
    experts = jnp.arange(N_EXPERTS, dtype=jnp.int32)
    pos = (jnp.sum(jnp.where(eid[..., None] == experts, seg_start, 0), axis=-1) + rank).reshape(-1)
    p = 2 * n + N_EXPERTS * tm
    tok = jnp.arange(2 * n, dtype=jnp.int32) // 2
    packed = jnp.stack([tok, lax.bitcast_convert_type(gates.reshape(-1), jnp.int32)], axis=-1)
    packed = jnp.zeros((p, 2), jnp.int32).at[pos].set(packed)
    tok_sorted = packed[:, 0]
    gate_sorted = lax.bitcast_convert_type(packed[:, 1], F32)
    tile_start = jnp.arange(p // tm, dtype=jnp.int32) * tm
    tile_valid = (tile_start < seg_end[-1]).astype(jnp.int32)
    tile_expert = jnp.sum((tile_start[:, None] >= seg_end[None, :]).astype(jnp.int32), axis=1)
    tile_expert = jnp.minimum(tile_expert, N_EXPERTS - 1)
    last_e = jnp.max(jnp.where(tile_valid > 0, tile_expert, 0))
    tile_expert = jnp.where(tile_valid > 0, tile_expert, last_e).astype(jnp.int32)
    ys = expert_ffn(x_f32, tok_sorted, gate_sorted.reshape(p, 1), tile_expert, tile_valid,
                    w_gate, w_up, w_down, layer, tm)
    pos2 = pos.reshape(n, 2)
    return (ys.at[pos2[:, 0]].get(mode="promise_in_bounds"),
            ys.at[pos2[:, 1]].get(mode="promise_in_bounds"))


C_HEAD = 128
HGRN_LEVELS = (1, 2, 4, 8, 16, 32)


def _cumsum_rows(x, row):
    axis = x.ndim - 2
    s = 1
    while s < x.shape[axis]:
        x = x + jnp.where(row >= s, pltpu.roll(x, s, axis=axis), 0.0)
        s *= 2
    return x


def _anchor_rows(b, m):
    n = b.shape[0]
    if m >= 8:
        parts = [jnp.broadcast_to(b[base + m:base + m + 1, :], (2 * m, b.shape[1]))
                 for base in range(0, n, 2 * m)]
        return parts[0] if len(parts) == 1 else jnp.concatenate(parts, axis=0)
    b3 = b.reshape(n // 8, 8, b.shape[1])
    sub = lax.broadcasted_iota(jnp.int32, b3.shape, 1)
    out = None
    for base in range(8 - 2 * m, -1, -2 * m):
        mid = jnp.broadcast_to(b3[:, base + m:base + m + 1, :], b3.shape)
        out = mid if out is None else jnp.where(sub < base + 2 * m, mid, out)
    return out.reshape(n, b.shape[1])


def _hgrn2_body(q_ref, f_ref, i_ref, g_ref, loglb_ref, log1mlb_ref, omlb_ref, gain_ref,
                o_ref, st_ref, *, hb, n_chunk):
    @pl.when(pl.program_id(2) == 0)
    def _():
        st_ref[...] = jnp.zeros_like(st_ref)

    row = lax.broadcasted_iota(jnp.int32, (CHUNK, hb * C_HEAD), 0)
    tt = lax.broadcasted_iota(jnp.int32, (CHUNK, CHUNK), 0)
    ss = lax.broadcasted_iota(jnp.int32, (CHUNK, CHUNK), 1)
    txs = tt ^ ss
    lvl_id = jnp.where(tt == ss, 0, -1)
    for li, m in enumerate(HGRN_LEVELS):
        lvl_id = jnp.where((ss < tt) & (txs >= m) & (txs < 2 * m), li + 1, lvl_id)

    heads = [slice(h * C_HEAD, (h + 1) * C_HEAD) for h in range(hb)]

    def head_scores(qm, km):
        return jnp.stack([_dot_nt(qm[:, hs], km[:, hs]) for hs in heads])

    def chunk_step(c, carry):
        rs = pl.ds(pl.multiple_of(c * CHUNK, CHUNK), CHUNK)
        q = q_ref[rs, :]
        f = f_ref[rs, :]
        v = i_ref[rs, :].astype(BF16)
        g = g_ref[rs, :]
        qs = q * jax.nn.sigmoid(q)
        log_sig = jnp.minimum(f, 0.0) - jnp.log1p(jnp.exp(-jnp.abs(f)))
        y = log1mlb_ref[...] + log_sig
        x = loglb_ref[...]
        log_f = jnp.maximum(x, y) + jnp.log1p(jnp.exp(-jnp.abs(x - y)))
        key = omlb_ref[...] * jax.nn.sigmoid(-f)
        b = _cumsum_rows(log_f, row)
        scores = jnp.where(lvl_id == 0, head_scores(qs.astype(BF16), key.astype(BF16)), 0.0)
        for li, m in enumerate(HGRN_LEVELS):
            upper = (row & m) != 0
            e = jnp.exp(jnp.where(upper, 1.0, -1.0) * (b - _anchor_rows(b, m)))
            xm = (jnp.where(upper, qs, key) * e).astype(BF16)
            scores = jnp.where(lvl_id == li + 1, head_scores(xm, xm), scores)
        scores = scores.astype(BF16)
        q_dec = (qs * jnp.exp(b)).astype(BF16)
        b_last = b[CHUNK - 1:CHUNK, :]
        k_dec = (key * jnp.exp(b_last - b)).astype(BF16)
        s_dec = jnp.exp(b_last)
        outs = []
        for h, hs in enumerate(heads):
            st = st_ref[h]
            o = _dot(scores[h], v[:, hs]) + _dot_nt(q_dec[:, hs], st.astype(BF16))
            st_ref[h] = st * s_dec[:, hs] + _dot_tn(v[:, hs], k_dec[:, hs])
            outs.append(o * lax.rsqrt(jnp.mean(o * o, axis=-1, keepdims=True) + RMS_EPS))
        o = jnp.concatenate(outs, axis=1) * gain_ref[...] * (g * jax.nn.sigmoid(g))
        o_ref[rs, :] = o.astype(o_ref.dtype)
        return carry

    lax.fori_loop(0, n_chunk, chunk_step, 0)


def hgrn2(cols, lb, norm_gain, bsz, seq, hb=8, tc=512):
    n, c4 = cols.shape
    c = c4 // 4
    wl = hb * C_HEAD
    nhb = c // wl
    nt = seq // tc
    lb = lb.astype(F32).reshape(1, c)
    loglb = jnp.log(lb)
    log1mlb = jnp.log1p(-lb)
    omlb = 1.0 - lb

    def col(k):
        return pl.BlockSpec((tc, wl), lambda b, h, t, k=k: (b * nt + t, k * nhb + h))

    vec = pl.BlockSpec((1, wl), lambda b, h, t: (0, h))
    return pl.pallas_call(
        functools.partial(_hgrn2_body, hb=hb, n_chunk=tc // CHUNK),
        grid=(bsz, nhb, nt),
        in_specs=[col(0), col(1), col(2), col(3), vec, vec, vec, vec],
        out_specs=pl.BlockSpec((tc, wl), lambda b, h, t: (b * nt + t, h)),
        out_shape=jax.ShapeDtypeStruct((n, c), BF16),
        scratch_shapes=[pltpu.VMEM((hb, C_HEAD, C_HEAD), F32)],
        compiler_params=_cparams(("parallel", "parallel", "arbitrary")),
        name="hgrn2",
    )(cols, cols, cols, cols, loglb, log1mlb, omlb, norm_gain.reshape(1, c))


B_WIDTH = 2048
B_HEAD = 64
B_HEADS = B_WIDTH // B_HEAD
B_LORA_PAD = 128
B_GATE_LORA = 256
B_COLS_PAD = 3 * B_WIDTH + 2 * B_LORA_PAD + B_GATE_LORA


def _head_pair_ones():
    r = lax.broadcasted_iota(jnp.int32, (LANES, LANES), 0) // B_HEAD
    c = lax.broadcasted_iota(jnp.int32, (LANES, LANES), 1) // B_HEAD
    return (r == c).astype(BF16)


def _head_sums(x, ones):
    outs = []
    for j in range(x.shape[1] // LANES):
        hi, lo = _split_bf16(x[:, j * LANES:(j + 1) * LANES])
        outs.append(_dot(hi, ones) + _dot(lo, ones))
    return outs[0] if len(outs) == 1 else jnp.concatenate(outs, axis=1)


def _dot3(a, wh, wl):
    ah, al = _split_bf16(a)
    return _dot(ah, wh) + (_dot(al, wh) + _dot(ah, wl))


def _softplus(z):
    return jnp.maximum(z, 0.0) + jnp.log1p(jnp.exp(-jnp.abs(z)))


def _rwkv_prep_body(c_ref, p_ref, mu_ref, w0_ref, a0_ref, kk_ref, ka_ref, rk_ref,
                    wuh_ref, wul_ref, auh_ref, aul_ref, gu_ref,
                    r_o, lw_o, k_o, v_o, kk_o, a_o, g_o, bonus_o, *, tiles_per_seq):
    i = pl.program_id(0)
    x = c_ref[...]
    tm = x.shape[0]
    row = lax.broadcasted_iota(jnp.int32, x.shape, 0)
    prev = jnp.where(i % tiles_per_seq == 0, 0.0, p_ref[7:8, :])
    shifted = jnp.where(row == 0, prev, pltpu.roll(x, 1, axis=0))
    mixed = x + (shifted - x) * mu_ref[...]
    w = B_WIDTH
    r = mixed[:, 0:w]
    k = mixed[:, w:2 * w]
    v = mixed[:, 2 * w:3 * w]
    wd = mixed[:, 3 * w:3 * w + B_LORA_PAD]
    ad = mixed[:, 3 * w + B_LORA_PAD:3 * w + 2 * B_LORA_PAD]
    gd = mixed[:, 3 * w + 2 * B_LORA_PAD:]
    wlog = -_softplus(-(w0_ref[...] + _dot3(jnp.tanh(wd), wuh_ref[...], wul_ref[...]))) - 0.5
    a = jax.nn.sigmoid(a0_ref[...] + _dot3(ad, auh_ref[...], aul_ref[...]))
    g = _dot(jax.nn.sigmoid(gd).astype(BF16), gu_ref[...])
    kk = k * kk_ref[...]
    ssq = _head_sums(kk * kk, _head_pair_ones())
    kk = kk / jnp.maximum(jnp.sqrt(ssq), 1e-12)
    k_mod = k * (1.0 + (a - 1.0) * ka_ref[...])
    g_o[...] = g
    bonus_o[...] = _head_sums(r * k_mod * rk_ref[...], _head_pair_ones()) * v
    for out_ref, val in ((r_o, r), (lw_o, -jnp.exp(wlog)), (k_o, k_mod), (v_o, v), (kk_o, kk), (a_o, a)):
        for h in range(B_HEADS):
            out_ref[h] = val[:, h * B_HEAD:(h + 1) * B_HEAD]


def _pad_rows(w, rows):
    return jnp.zeros((rows, w.shape[1]), w.dtype).at[:w.shape[0]].set(w)


def rwkv_prep(cols, mu_pad, w0, w_up, a0, a_up, g_up, k_k, k_a, r_k, seq, tm=128):
    n, cw = cols.shape
    w = B_WIDTH
    wuh, wul = _split_bf16(_pad_rows(w_up, B_LORA_PAD))
    auh, aul = _split_bf16(_pad_rows(a_up, B_LORA_PAD))
    row = pl.BlockSpec((tm, w), lambda i: (i, 0))
    heads = pl.BlockSpec((B_HEADS, tm, B_HEAD), lambda i: (0, i, 0))
    vec = pl.BlockSpec((1, w), lambda i: (0, 0))
    lora = pl.BlockSpec((B_LORA_PAD, w), lambda i: (0, 0))
    outs = pl.pallas_call(
        functools.partial(_rwkv_prep_body, tiles_per_seq=seq // tm),
        grid=(n // tm,),
        in_specs=[pl.BlockSpec((tm, cw), lambda i: (i, 0)),
                  pl.BlockSpec((8, cw), lambda i: (jnp.maximum(i * (tm // 8) - 1, 0), 0)),
                  pl.BlockSpec((1, cw), lambda i: (0, 0)),
                  vec, vec, vec, vec, vec, lora, lora, lora, lora,
                  pl.BlockSpec((B_GATE_LORA, w), lambda i: (0, 0))],
        out_specs=[heads] * 6 + [row] * 2,
        out_shape=[jax.ShapeDtypeStruct((B_HEADS, n, B_HEAD), F32)] * 6 + [jax.ShapeDtypeStruct((n, w), F32)] * 2,
        compiler_params=_cparams(("parallel",)),
        name="rwkv_prep",
    )(cols, cols, mu_pad.reshape(1, cw), w0.reshape(1, w), a0.reshape(1, w), k_k.reshape(1, w),
      k_a.reshape(1, w), r_k.reshape(1, w), wuh, wul, auh, aul, g_up.astype(BF16))
    return outs


def _rwkv_scan_body(r_ref, lw_ref, k_ref, v_ref, kk_ref, a_ref, y_ref, st_ref, qm_ref, ys_ref, *, hb, n_chunk):
    @pl.when(pl.program_id(2) == 0)
    def _():
        st_ref[...] = jnp.zeros_like(st_ref)

    L = CHUNK
    row = lax.broadcasted_iota(jnp.int32, (hb, L, B_HEAD), 1)
    tt = lax.broadcasted_iota(jnp.int32, (L, L), 0)
    ss = lax.broadcasted_iota(jnp.int32, (L, L), 1)
    strict = ss < tt
    incl = ss <= tt
    eye = ss == tt
    eye_f = eye.astype(F32)

    def local_step(c, carry):
        sl = pl.ds(pl.multiple_of(c * L, L), L)
        r = r_ref[:, sl, :]
        lw = lw_ref[:, sl, :]
        k = k_ref[:, sl, :]
        v = v_ref[:, sl, :]
        kk = kk_ref[:, sl, :]
        a = a_ref[:, sl, :]
        lp = _cumsum_rows(lw, row)
        p = jnp.exp(lp)
        inv_p = jnp.exp(-lp)
        at = -kk * jnp.exp(lp - lw)
        bt = kk * a * inv_p
        kt = k * inv_p
        rt = r * p
        p_last = p[:, L - 1:L, :]
        bp = bt * p_last
        kp = kt * p_last
        ar = jnp.concatenate([at, rt], axis=1)
        m_b = _bdot_nt(ar, bt)
        m_k = _bdot_nt(ar, kt)
        nmat = jnp.where(strict, m_b[:, :L], 0.0)
        a_rb = jnp.where(incl, m_b[:, L:], 0.0)
        g = _bdot(jnp.where(strict, m_k[:, :L], 0.0), v)
        tmat = eye_f + nmat
        npow = nmat
        lvl = 2
        while lvl < L:
            npow = _bdot(npow, npow)
            tmat = tmat + _bdot(npow, tmat)
            lvl *= 2
        w = _bdot(tmat, at)
        u = _bdot(tmat, g)
        qe = rt + _bdot(a_rb, w)
        yl = _bdot(a_rb, u) + _bdot(jnp.where(incl, m_k[:, L:], 0.0), v)
        qm_ref[:, c, 0:L, :] = qe
        ys_ref[:, c, 0:L, :] = yl
        for h in range(hb):
            qm_ref[h, c, L:2 * L, :] = _dot_tn(bp[h], w[h]) + jnp.where(eye, p_last[h], 0.0)
            ys_ref[h, c, L:2 * L, :] = _dot_tn(bp[h], u[h]) + _dot_tn(kp[h], v[h])
        return carry

    lax.fori_loop(0, n_chunk, local_step, 0)

    def serial_step(c, carry):
        sl = pl.ds(pl.multiple_of(c * L, L), L)
        out = _bdot(qm_ref[:, c], st_ref[...]) + ys_ref[:, c]
        y_ref[:, sl, :] = out[:, :L]
        st_ref[...] = out[:, L:]
        return carry

    lax.fori_loop(0, n_chunk, serial_step, 0)


def rwkv_scan(r, lw, k, v, kk, a, bsz, seq, hb=16, tc=256):
    nh, n, hd = r.shape
    tc = min(tc, seq)
    nt = seq // tc
    n_chunk = tc // CHUNK
    blk = pl.BlockSpec((hb, tc, hd), lambda b, h, t: (h, b * nt + t, 0))
    return pl.pallas_call(
        functools.partial(_rwkv_scan_body, hb=hb, n_chunk=n_chunk),
        grid=(bsz, nh // hb, nt),
        in_specs=[blk] * 6,
        out_specs=blk,
        out_shape=jax.ShapeDtypeStruct((nh, n, hd), F32),
        scratch_shapes=[pltpu.VMEM((hb, hd, hd), F32),
                        pltpu.VMEM((hb, n_chunk, 2 * CHUNK, hd), F32),
                        pltpu.VMEM((hb, n_chunk, 2 * CHUNK, hd), F32)],
        compiler_params=_cparams(("parallel", "parallel", "arbitrary")),
        name="rwkv_scan",
    )(r, lw, k, v, kk, a)


def _rwkv_post_body(y_ref, g_ref, bonus_ref, lg_ref, lb_ref, o_ref):
    ones = _head_pair_ones()
    y = jnp.concatenate([y_ref[h] for h in range(B_HEADS)], axis=1)
    mean = _head_sums(y, ones) * (1.0 / B_HEAD)
    yc = y - mean
    var = _head_sums(yc * yc, ones) * (1.0 / B_HEAD)
    yn = yc * lax.rsqrt(var + GN_EPS) * lg_ref[...] + lb_ref[...]
    o_ref[...] = ((yn + bonus_ref[...]) * g_ref[...]).astype(o_ref.dtype)


def rwkv_post(y_heads, g, bonus, ln_gain, ln_bias, tm=256):
    n, w = g.shape
    row = pl.BlockSpec((tm, w), lambda i: (i, 0))
    vec = pl.BlockSpec((1, w), lambda i: (0, 0))
    return pl.pallas_call(
        _rwkv_post_body,
        grid=(n // tm,),
        in_specs=[pl.BlockSpec((B_HEADS, tm, B_HEAD), lambda i: (0, i, 0)), row, row, vec, vec],
        out_specs=row,
        out_shape=jax.ShapeDtypeStruct((n, w), BF16),
        compiler_params=_cparams(("parallel",)),
        name="rwkv_post",
    )(y_heads, g, bonus, ln_gain.reshape(1, w), ln_bias.reshape(1, w))


def mixer_rwkv7(cols_b, mu_pad, w0, w_up, a0, a_up, g_up, k_k, k_a, r_k, ln_gain, ln_bias, bsz, seq):
    r, lw, k, v, kk, a, g, bonus = rwkv_prep(cols_b, mu_pad, w0, w_up, a0, a_up, g_up, k_k, k_a,
                                             r_k.reshape(-1), seq)
    y_heads = rwkv_scan(r, lw, k, v, kk, a, bsz, seq)
    return rwkv_post(y_heads, g, bonus, ln_gain, ln_bias)


A_Q_RANK = 768
A_KV_RANK = 512
IDX_DIM = 64
IDX_HEADS = 16
A_HEADS = 16
A_HEAD_DIM = 128
A_COLS_PAD = A_Q_RANK + A_KV_RANK + 2 * LANES
INT_MIN = -2 ** 31


def _dsa_prep_body(c_ref, qn_ref, kn_ref, g_ref, b_ref, q_o, c_o, k_o, w_o):
    x = c_ref[...]
    ql = x[:, :A_Q_RANK]
    q_o[...] = (ql * lax.rsqrt(jnp.mean(ql * ql, axis=-1, keepdims=True) + RMS_EPS) * qn_ref[...]).astype(q_o.dtype)
    kv = x[:, A_Q_RANK:A_Q_RANK + A_KV_RANK]
    c_o[...] = (kv * lax.rsqrt(jnp.mean(kv * kv, axis=-1, keepdims=True) + RMS_EPS) * kn_ref[...]).astype(c_o.dtype)
    o = A_Q_RANK + A_KV_RANK
    ki = x[:, o:o + LANES]
    valid = lax.broadcasted_iota(jnp.int32, ki.shape, 1) < IDX_DIM
    mu = jnp.sum(jnp.where(valid, ki, 0.0), axis=-1, keepdims=True) * (1.0 / IDX_DIM)
    kc = jnp.where(valid, ki - mu, 0.0)
    var = jnp.sum(kc * kc, axis=-1, keepdims=True) * (1.0 / IDX_DIM)
    kn = kc * lax.rsqrt(var + LN_EPS) * g_ref[...] + b_ref[...]
    k_o[...] = kn[:, :IDX_DIM].astype(k_o.dtype)
    w_o[...] = x[:, o + LANES:o + 2 * LANES] * (IDX_HEADS ** -0.5 * IDX_DIM ** -0.5)


def dsa_prep(cols, q_norm, kv_norm, k_gain, k_bias, tm=512):
    n, cw = cols.shape
    pad = jnp.zeros((LANES - IDX_DIM,), F32)
    return pl.pallas_call(
        _dsa_prep_body,
        grid=(n // tm,),
        in_specs=[pl.BlockSpec((tm, cw), lambda i: (i, 0)),
                  pl.BlockSpec((1, A_Q_RANK), lambda i: (0, 0)),
                  pl.BlockSpec((1, A_KV_RANK), lambda i: (0, 0)),
                  pl.BlockSpec((1, LANES), lambda i: (0, 0)),
                  pl.BlockSpec((1, LANES), lambda i: (0, 0))],
        out_specs=[pl.BlockSpec((tm, A_Q_RANK), lambda i: (i, 0)),
                   pl.BlockSpec((tm, A_KV_RANK), lambda i: (i, 0)),
                   pl.BlockSpec((tm, IDX_DIM), lambda i: (i, 0)),
                   pl.BlockSpec((tm, LANES), lambda i: (i, 0))],
        out_shape=[jax.ShapeDtypeStruct((n, A_Q_RANK), BF16), jax.ShapeDtypeStruct((n, A_KV_RANK), BF16),
                   jax.ShapeDtypeStruct((n, IDX_DIM), BF16), jax.ShapeDtypeStruct((n, LANES), F32)],
        compiler_params=_cparams(("parallel",)),
        name="dsa_prep",
    )(cols, q_norm.reshape(1, -1), kv_norm.reshape(1, -1),
      jnp.concatenate([k_gain, pad]).reshape(1, LANES), jnp.concatenate([k_bias, pad]).reshape(1, LANES))


def _head_mm_body(a_ref, w_ref, o_ref, *, scale):
    o_ref[0] = (_dot(a_ref[...], w_ref[0]) * scale).astype(o_ref.dtype)


def head_matmul(a, w, scale, tm=1024):
    n = a.shape[0]
    nh, k, f = w.shape
    tm = min(tm, n)
    return pl.pallas_call(
        functools.partial(_head_mm_body, scale=scale),
        grid=(nh, n // tm),
        in_specs=[pl.BlockSpec((tm, k), lambda h, i: (i, h)),
                  pl.BlockSpec((1, k, f), lambda h, i: (h, 0, 0))],
        out_specs=pl.BlockSpec((1, tm, f), lambda h, i: (h, i, 0)),
        out_shape=jax.ShapeDtypeStruct((nh, n, f), BF16),
        compiler_params=_cparams(("parallel", "parallel")),
        name="head_matmul",
    )(a, w)


DSA_KT = 256
DSA_AT = 1024
DSA_HEAD_GROUP = 4


def _dsa_attend_body(qi_ref, wi_ref, qa_ref, ck_ref, kx_ref, wuv_ref, sl_ref, o_ref,
                     keys_ref, m_ref, l_ref, acc_ref, *, top_k, seq, at_tile):
    j = pl.program_id(1)
    at_n = ((j + 1) * Q_BLOCK + at_tile - 1) // at_tile
    kt_n = at_n * (at_tile // DSA_KT)
    nq = Q_BLOCK
    nh = A_HEADS
    qrow = lax.broadcasted_iota(jnp.int32, (nq, 1), 0)
    qpos = j * Q_BLOCK + qrow
    limit = (qpos // CHUNK + 1) * CHUNK
    lane = lax.broadcasted_iota(jnp.int32, (nq, DSA_KT), 1)
    lane_a = lax.broadcasted_iota(jnp.int32, (nq, at_tile), 1)

    qi = qi_ref[...].reshape(nh * nq, IDX_DIM)
    wi = wi_ref[...]

    def score_tile(kt, carry):
        starts = [pl.multiple_of(kt * at_tile + t * DSA_KT, DSA_KT) for t in range(at_tile // DSA_KT)]
        rels = [_dot_nt(qi, kx_ref[pl.ds(k0, DSA_KT), :]) for k0 in starts]
        for k0, rel in zip(starts, rels):
            score = jnp.sum(jnp.maximum(rel, 0.0).reshape(nh, nq, DSA_KT) * wi, axis=0)
            bits = lax.bitcast_convert_type(score, jnp.int32)
            key = bits ^ ((bits >> 31) & 0x7FFFFFFF)
            keys_ref[:, pl.ds(k0, DSA_KT)] = jnp.where(k0 + lane < limit, key, INT_MIN)
        return carry

    lax.fori_loop(0, at_n, score_tile, 0)

    def count_keys(pred):
        def count_tile(kt, cnt):
            k0 = pl.multiple_of(kt * at_tile, at_tile)
            hit = pred(keys_ref[:, pl.ds(k0, at_tile)], k0).astype(jnp.int32)
            for t in range(at_tile // LANES):
                cnt = cnt + hit[:, t * LANES:(t + 1) * LANES]
            return cnt

        cnt = lax.fori_loop(0, at_n, count_tile, jnp.zeros((nq, LANES), jnp.int32))
        return jnp.sum(cnt, axis=-1, keepdims=True)

    def bit_step(i, carry):
        ans, n_ge = carry
        cand = ans | lax.shift_left(jnp.int32(1), 31 - i)
        cand_s = cand ^ INT_MIN
        total = count_keys(lambda key, k0: key >= cand_s)
        ok = total >= top_k
        return jnp.where(ok, cand, ans), jnp.where(ok, total, n_ge)

    zero = jnp.zeros((nq, 1), jnp.int32)
    ans, n_ge = lax.fori_loop(0, 32, bit_step, (zero, zero))
    thr = ans ^ INT_MIN

    excess = (ans != 0) & (n_ge > top_k)
    idx_bits = max(1, (seq - 1).bit_length())

    def tie_cut():
        need = top_k - count_keys(lambda key, k0: key > thr)

        def idx_step(i, cut):
            cand = cut | lax.shift_left(jnp.int32(1), idx_bits - 1 - i)
            below = count_keys(lambda key, k0: (key == thr) & (k0 + lane_a < cand))
            return jnp.where(below < need, cand, cut)

        return lax.fori_loop(0, idx_bits, idx_step, zero)

    cut = lax.cond(jnp.max(excess.astype(jnp.int32)) > 0, tie_cut, lambda: zero)
    cut = jnp.where(excess, cut, seq)

    m_ref[...] = jnp.full_like(m_ref, -jnp.inf)
    l_ref[...] = jnp.zeros_like(l_ref)
    acc_ref[...] = jnp.zeros_like(acc_ref)
    slopes = sl_ref[...]
    gh = DSA_HEAD_GROUP

    def attend_tile(kt, carry):
        k0 = pl.multiple_of(kt * at_tile, at_tile)
        ck = ck_ref[pl.ds(k0, at_tile), :]
        key = keys_ref[:, pl.ds(k0, at_tile)]
        kidx = k0 + lane_a
        keep = ((key > thr) | ((key == thr) & (kidx <= cut))) & (key > INT_MIN)
        neg_dist = jnp.where(keep, -jnp.abs(qpos - kidx).astype(F32), -jnp.inf)
        def qk(g):
            return _dot_nt(qa_ref[g * gh:(g + 1) * gh].reshape(gh * nq, A_KV_RANK), ck)

        s_next = qk(0)
        for g in range(nh // gh):
            s = s_next
            if g + 1 < nh // gh:
                s_next = qk(g + 1)
            rows = pl.ds(g * gh * nq, gh * nq)
            s = s.reshape(gh, nq, at_tile)
            s = (s + slopes[g * gh:(g + 1) * gh] * neg_dist).reshape(gh * nq, at_tile)
            m_old = m_ref[rows, :]
            m_new = jnp.maximum(m_old, jnp.max(s, axis=-1, keepdims=True))
            m_safe = jnp.where(m_new == -jnp.inf, 0.0, m_new)
            p = jnp.exp(s - m_safe)
            alpha = jnp.exp(m_old - m_safe)
            l_ref[rows, :] = alpha * l_ref[rows, :] + jnp.sum(p, axis=-1, keepdims=True)
            acc_ref[rows, :] = alpha * acc_ref[rows, :] + _dot(p.astype(BF16), ck)
            m_ref[rows, :] = m_new
        return carry

    lax.fori_loop(0, at_n, attend_tile, 0)
    o_lat = (acc_ref[...] / l_ref[...]).astype(BF16).reshape(nh, nq, A_KV_RANK)
    for h in range(nh):
        o_ref[:, h * A_HEAD_DIM:(h + 1) * A_HEAD_DIM] = _dot(o_lat[h], wuv_ref[h]).astype(o_ref.dtype)


def dsa_attend(qi, wi, qa, ckv, kx, w_uv, bsz, seq):
    n = ckv.shape[0]
    nblk = seq // Q_BLOCK
    top_k = min(TOPK_MAX, seq // 4)
    start = 2.0 ** (-8.0 / A_HEADS)
    slopes = jnp.asarray([start ** (h + 1) for h in range(A_HEADS)], F32).reshape(A_HEADS, 1, 1)

    def qblk(last):
        return pl.BlockSpec((A_HEADS, Q_BLOCK, last), lambda b, j: (0, b * nblk + j, 0))

    return pl.pallas_call(
        functools.partial(_dsa_attend_body, top_k=top_k, seq=seq, at_tile=min(DSA_AT, seq)),
        grid=(bsz, nblk),
        in_specs=[qblk(IDX_DIM), qblk(1), qblk(A_KV_RANK),
                  pl.BlockSpec((seq, A_KV_RANK), lambda b, j: (b, 0), pipeline_mode=pl.Buffered(1)),
                  pl.BlockSpec((seq, IDX_DIM), lambda b, j: (b, 0)),
                  pl.BlockSpec((A_HEADS, A_KV_RANK, A_HEAD_DIM), lambda b, j: (0, 0, 0)),
                  pl.BlockSpec((A_HEADS, 1, 1), lambda b, j: (0, 0, 0))],
        out_specs=pl.BlockSpec((Q_BLOCK, A_HEADS * A_HEAD_DIM), lambda b, j: (b * nblk + j, 0)),
        out_shape=jax.ShapeDtypeStruct((n, A_HEADS * A_HEAD_DIM), BF16),
        scratch_shapes=[pltpu.VMEM((Q_BLOCK, seq), jnp.int32),
                        pltpu.VMEM((A_HEADS * Q_BLOCK, 1), F32),
                        pltpu.VMEM((A_HEADS * Q_BLOCK, 1), F32),
                        pltpu.VMEM((A_HEADS * Q_BLOCK, A_KV_RANK), F32)],
        compiler_params=_cparams(("parallel", "arbitrary")),
        name="dsa_attend",
    )(qi, wi, qa, ckv, kx, w_uv, slopes)


def mixer_dsa(cols_a, q_norm, kv_norm, w_uq, w_uk, w_uv, w_idx_q, k_gain, k_bias, bsz, seq):
    n = cols_a.shape[0]
    q_lat, c_kv, k_idx, w_idx = dsa_prep(cols_a, q_norm, kv_norm, k_gain, k_bias)
    q = matmul(q_lat, w_uq.reshape(A_Q_RANK, -1).astype(BF16), out_dtype=BF16)
    qa = head_matmul(q, w_uk.transpose(1, 2, 0).astype(BF16), A_HEAD_DIM ** -0.5)
    qi = matmul(q_lat, w_idx_q.reshape(A_Q_RANK, -1).astype(BF16), out_dtype=BF16)
    qi = qi.reshape(n, IDX_HEADS, IDX_DIM).transpose(1, 0, 2)
    wi = w_idx[:, :IDX_HEADS].T.reshape(IDX_HEADS, n, 1)
    return dsa_attend(qi, wi, qa, c_kv, k_idx, w_uv.transpose(1, 0, 2).astype(BF16), bsz, seq)


A_SPLITS = (A_Q_RANK, A_KV_RANK, IDX_DIM, IDX_HEADS)
A_COLS = sum(A_SPLITS)
DECAY_LORA = 96
AAA_LORA = 96


def _pad_cols(w, sizes, padded):
    out, o = [], 0
    for s, p in zip(sizes, padded):
        piece = w[..., o:o + s]
        if p > s:
            piece = jnp.concatenate([piece, jnp.zeros(w.shape[:-1] + (p - s,), w.dtype)], axis=-1)
        out.append(piece)
        o += s
    return jnp.concatenate(out, axis=-1)


def kernel(x, ln1_gain, ln1_bias, ln2_gain, ln2_bias, w_in_even, a_q_norm, a_kv_norm, a_w_uq, a_w_uk, a_w_uv, a_w_idx_q, a_idx_k_gain, a_idx_k_bias, b_mu, b_w0, b_w_up, b_a0, b_a_up, b_g_up, b_k_k, b_k_a, b_r_k, b_ln_gain, b_ln_bias, w_out_even, w_in_odd, c_lb_logits, c_norm_gain, w_out_odd, router_group, router_group_bias, router_expert, router_expert_bias, moe_w_gate, moe_w_up, moe_w_down):
    bsz, seq, d = x.shape
    n = bsz * seq
    lb_table = jnp.cumsum(jax.nn.softmax(c_lb_logits.astype(F32), axis=0), axis=0)
    lb_table = lb_table - lb_table[:1]
    a_sizes, a_padded = A_SPLITS, (A_Q_RANK, A_KV_RANK, LANES, LANES)
    b_sizes = (3 * B_WIDTH, DECAY_LORA, AAA_LORA, B_GATE_LORA)
    b_padded = (3 * B_WIDTH, B_LORA_PAD, B_LORA_PAD, B_GATE_LORA)
    xf = x.reshape(n, d)
    xb = xf.astype(BF16)
    for layer in range(DEPTH):
        j = layer // 2
        if layer % 2 == 0:
            w_in = w_in_even[j]
            w_a = _pad_cols(w_in[:, :A_COLS], a_sizes, a_padded).astype(BF16)
            w_b = _pad_cols(w_in[:, A_COLS:], b_sizes, b_padded).astype(BF16)
            cols_a = matmul(xb, w_a, tn=512)
            cols_b = matmul(xb, w_b, tn=512)
            y_a = mixer_dsa(cols_a, a_q_norm[j], a_kv_norm[j], a_w_uq[j], a_w_uk[j], a_w_uv[j],
                            a_w_idx_q[j], a_idx_k_gain[j], a_idx_k_bias[j], bsz, seq)
            y_b = mixer_rwkv7(cols_b, _pad_cols(b_mu[j], b_sizes, b_padded), b_w0[j], b_w_up[j], b_a0[j],
                              b_a_up[j], b_g_up[j], b_k_k[j], b_k_a[j], b_r_k[j], b_ln_gain[j], b_ln_bias[j],
                              bsz, seq)
            mixed = jnp.concatenate([y_a, y_b], axis=-1)
            w_out = w_out_even[j]
        else:
            cols = matmul_stacked_w(xb, w_in_odd, j)
            mixed = hgrn2(cols, lb_table[j], c_norm_gain[j], bsz, seq)
            w_out = w_out_odd[j]
        xf, xb = proj_residual_ln(mixed, w_out.astype(BF16), xf, ln1_gain[layer], ln1_bias[layer])
        y0, y1 = hier_moe(xf, router_group[layer], router_group_bias[layer], router_expert[layer],
                          router_expert_bias[layer], moe_w_gate, moe_w_up, moe_w_down, layer)
        xf, xb = add_ln(xf, y0, y1, ln2_gain[layer], ln2_bias[layer])
    return xf.reshape(bsz, seq, d)
```

```python
import functools

import jax
import jax.numpy as jnp
from jax import lax
from jax.experimental import pallas as pl
from jax.experimental.pallas import tpu as pltpu

F32 = jnp.float32
BF16 = jnp.bfloat16

DEPTH = 4
ALPHA = (2 * DEPTH) ** 0.25
LN_EPS = 1e-5
RMS_EPS = 1e-6
GN_EPS = 64e-5
CHUNK = 64
Q_BLOCK = 128
TOPK_MAX = 256
N_GROUPS = 4
EXPERTS_PER_GROUP = 8
N_EXPERTS = N_GROUPS * EXPERTS_PER_GROUP

LANES = 128
VMEM_LIMIT = 56 * 1024 * 1024


def _cparams(sem):
    return pltpu.CompilerParams(dimension_semantics=sem, vmem_limit_bytes=VMEM_LIMIT)


def _dot(a, b):
    return jnp.dot(a, b, preferred_element_type=F32)


def _dot_nt(a, b):
    return lax.dot_general(a, b, (((1,), (1,)), ((), ())), preferred_element_type=F32)


def _dot_tn(a, b):
    return lax.dot_general(a, b, (((0,), (0,)), ((), ())), preferred_element_type=F32)


def _bdot(a, b):
    return lax.dot_general(a, b, (((2,), (1,)), ((0,), (0,))), preferred_element_type=F32)


def _bdot_nt(a, b):
    return lax.dot_general(a, b, (((2,), (2,)), ((0,), (0,))), preferred_element_type=F32)


def _split_bf16(x):
    hi = x.astype(BF16)
    lo = (x - hi.astype(F32)).astype(BF16)
    return hi, lo


def _mm_body(a_ref, b_ref, o_ref):
    o_ref[...] = _dot(a_ref[...].astype(BF16), b_ref[...]).astype(o_ref.dtype)


def matmul(a, b, out_dtype=F32, tm=512, tn=1024):
    m, k = a.shape
    _, n = b.shape
    tm = min(tm, m)
    tn = min(tn, n)
    assert m % tm == 0 and n % tn == 0
    return pl.pallas_call(
        _mm_body,
        grid=(n // tn, m // tm),
        in_specs=[pl.BlockSpec((tm, k), lambda j, i: (i, 0)),
                  pl.BlockSpec((k, tn), lambda j, i: (0, j))],
        out_specs=pl.BlockSpec((tm, tn), lambda j, i: (i, j)),
        out_shape=jax.ShapeDtypeStruct((m, n), out_dtype),
        compiler_params=_cparams(("parallel", "parallel")),
        name="matmul",
    )(a, b)


def _mm_wcast_body(a_ref, w_ref, o_ref, wb_ref):
    @pl.when(pl.program_id(1) == 0)
    def _():
        wb_ref[...] = w_ref[0].astype(BF16)

    o_ref[...] = _dot(a_ref[...], wb_ref[...]).astype(o_ref.dtype)


def matmul_stacked_w(a, w, layer, out_dtype=F32, tm=512, tn=1024):
    m, k = a.shape
    n = w.shape[2]
    tm = min(tm, m)
    return pl.pallas_call(
        _mm_wcast_body,
        grid=(n // tn, m // tm),
        in_specs=[pl.BlockSpec((tm, k), lambda j, i: (i, 0)),
                  pl.BlockSpec((1, k, tn), lambda j, i: (layer, 0, j), pipeline_mode=pl.Buffered(1))],
        out_specs=pl.BlockSpec((tm, tn), lambda j, i: (i, j)),
        out_shape=jax.ShapeDtypeStruct((m, n), out_dtype),
        scratch_shapes=[pltpu.VMEM((k, tn), BF16)],
        compiler_params=_cparams(("parallel", "arbitrary")),
        name="matmul_stacked_w",
    )(a, w)


def _ln_rows(z, gain, bias):
    mu = jnp.mean(z, axis=-1, keepdims=True)
    zc = z - mu
    var = jnp.mean(zc * zc, axis=-1, keepdims=True)
    return zc * lax.rsqrt(var + LN_EPS) * gain + bias


def _add_ln_body(x_ref, y0_ref, y1_ref, g_ref, b_ref, o_ref, ob_ref):
    z = ALPHA * x_ref[...] + (y0_ref[...].astype(F32) + y1_ref[...].astype(F32))
    y = _ln_rows(z, g_ref[...], b_ref[...])
    o_ref[...] = y
    ob_ref[...] = y.astype(BF16)


def add_ln(x, y0, y1, gain, bias, tm=256):
    m, d = x.shape
    row = pl.BlockSpec((tm, d), lambda i: (i, 0))
    vec = pl.BlockSpec((1, d), lambda i: (0, 0))
    return pl.pallas_call(
        _add_ln_body,
        grid=(m // tm,),
        in_specs=[row, row, row, vec, vec],
        out_specs=[row, row],
        out_shape=[jax.ShapeDtypeStruct((m, d), F32), jax.ShapeDtypeStruct((m, d), BF16)],
        compiler_params=_cparams(("parallel",)),
        name="add_ln",
    )(x, y0, y1, gain.reshape(1, d), bias.reshape(1, d))


def _route(x, wh_ref, wl_ref, b_ref, tri_ref, eid_ref, gate_ref, rank_ref, cnt_ref, base_ref):
    xh, xl = _split_bf16(x)
    wh = wh_ref[...]
    logits = _dot(xh, wh) + (_dot(xl, wh) + _dot(xh, wl_ref[...])) + b_ref[...]
    lane = lax.broadcasted_iota(jnp.int32, logits.shape, 1)
    neg = jnp.float32(-jnp.inf)
    is_g = lane < N_GROUPS
    gl = jnp.where(is_g, logits, neg)
    gmax = jnp.max(gl, axis=-1, keepdims=True)
    g_idx = jnp.min(jnp.where(gl == gmax, lane, LANES), axis=-1, keepdims=True)
    g_p = 1.0 / jnp.sum(jnp.exp(gl - gmax), axis=-1, keepdims=True)
    e_lane = lane - N_GROUPS
    in_grp = (e_lane >= g_idx * EXPERTS_PER_GROUP) & (e_lane < (g_idx + 1) * EXPERTS_PER_GROUP)
    el = jnp.where(in_grp, logits, neg)
    emax = jnp.max(el, axis=-1, keepdims=True)
    pe = jnp.exp(el - emax)
    pe = pe / jnp.sum(pe, axis=-1, keepdims=True)
    v1 = jnp.max(pe, axis=-1, keepdims=True)
    i1 = jnp.min(jnp.where(in_grp & (pe == v1), lane, LANES), axis=-1, keepdims=True)
    rest = in_grp & (lane != i1)
    pe2 = jnp.where(rest, pe, -1.0)
    v2 = jnp.max(pe2, axis=-1, keepdims=True)
    i2 = jnp.min(jnp.where(rest & (pe2 == v2), lane, LANES), axis=-1, keepdims=True)
    den = v1 + v2
    eid_ref[...] = jnp.where(lane == 0, i1 - N_GROUPS, i2 - N_GROUPS)
    gate_ref[...] = jnp.where(lane == 0, g_p * (v1 / den), g_p * (v2 / den))

    @pl.when(pl.program_id(0) == 0)
    def _():
        base_ref[...] = jnp.zeros_like(base_ref)

    hit1 = lane == i1
    hit2 = lane == i2
    oh1 = hit1.astype(BF16)
    oh2 = hit2.astype(BF16)
    tri = tri_ref[...]
    cnt1 = jnp.sum(oh1.astype(F32), axis=0, keepdims=True)
    cnt2 = jnp.sum(oh2.astype(F32), axis=0, keepdims=True)
    base = base_ref[...]
    r1 = jnp.sum(jnp.where(hit1, base + _dot(tri, oh1), 0.0), axis=-1, keepdims=True)
    r2 = jnp.sum(jnp.where(hit2, base + cnt1 + _dot(tri, oh2), 0.0), axis=-1, keepdims=True)
    rank_ref[...] = jnp.where(lane == 0, r1, r2).astype(jnp.int32)
    base_ref[...] = base + cnt1 + cnt2
    cnt_ref[...] = (base + cnt1 + cnt2).astype(jnp.int32)


def _residual_ln_route_body(x_ref, mix_ref, g_ref, b_ref, wh_ref, wl_ref, rb_ref, tri_ref,
                            o_ref, ob_ref, eid_ref, gate_ref, rank_ref, cnt_ref, base_ref):
    y = _ln_rows(ALPHA * x_ref[...] + mix_ref[...], g_ref[...], b_ref[...])
    o_ref[...] = y
    ob_ref[...] = y.astype(BF16)
    _route(y, wh_ref, wl_ref, rb_ref, tri_ref, eid_ref, gate_ref, rank_ref, cnt_ref, base_ref)


def residual_ln_route(x, mix, gain, bias, wg, bg, we, be, tm=256):
    m, d = x.shape
    tm = min(tm, m)
    w = jnp.zeros((d, LANES), F32).at[:, :N_GROUPS].set(wg).at[:, N_GROUPS:N_GROUPS + N_EXPERTS].set(we)
    b = jnp.zeros((1, LANES), F32).at[0, :N_GROUPS].set(bg).at[0, N_GROUPS:N_GROUPS + N_EXPERTS].set(be)
    wh, wl = _split_bf16(w)
    tri = jnp.tril(jnp.ones((tm, tm), BF16), k=-1)
    full = pl.BlockSpec((tm, d), lambda i: (i, 0))
    dvec = pl.BlockSpec((1, d), lambda i: (0, 0))
    row = pl.BlockSpec((tm, LANES), lambda i: (i, 0))
    wspec = pl.BlockSpec((d, LANES), lambda i: (0, 0))
    vec = pl.BlockSpec((1, LANES), lambda i: (0, 0))
    xf, xb, eid, gate, rank, cnt = pl.pallas_call(
        _residual_ln_route_body,
        grid=(m // tm,),
        in_specs=[full, full, dvec, dvec, wspec, wspec, vec, pl.BlockSpec((tm, tm), lambda i: (0, 0))],
        out_specs=[full, full, row, row, row, vec],
        out_shape=[jax.ShapeDtypeStruct((m, d), F32), jax.ShapeDtypeStruct((m, d), BF16),
                   jax.ShapeDtypeStruct((m, LANES), jnp.int32), jax.ShapeDtypeStruct((m, LANES), F32),
                   jax.ShapeDtypeStruct((m, LANES), jnp.int32), jax.ShapeDtypeStruct((1, LANES), jnp.int32)],
        scratch_shapes=[pltpu.VMEM((1, LANES), F32)],
        compiler_params=_cparams(("arbitrary",)),
        name="residual_ln_route",
    )(x, mix, gain.reshape(1, d), bias.reshape(1, d), wh, wl, b, tri)
    return xf, xb, (eid[:, :2], gate[:, :2], rank[:, :2], cnt[0, N_GROUPS:N_GROUPS + N_EXPERTS])


def _expert_body(te_ref, tv_ref, tok_ref, x_hbm, gate_ref, wg_ref, wu_ref, wd_ref, o_ref,
                 wgb_ref, wub_ref, wdb_ref, xbuf_ref, sem_ref):
    i = pl.program_id(0)
    n_tiles = pl.num_programs(0)
    tm = xbuf_ref.shape[1]

    def start_gather(tile, slot):
        def issue(r, carry):
            tok = tok_ref[tile * tm + r]
            pltpu.make_async_copy(x_hbm.at[pl.ds(tok, 1)], xbuf_ref.at[slot, pl.ds(r, 1)],
                                  sem_ref.at[slot]).start()
            return carry

        lax.fori_loop(0, tm, issue, 0, unroll=8)

    @pl.when((i == 0) & (tv_ref[0] > 0))
    def _():
        start_gather(0, 0)

    nxt = jnp.minimum(i + 1, n_tiles - 1)

    @pl.when((i + 1 < n_tiles) & (tv_ref[nxt] > 0))
    def _():
        start_gather(i + 1, (i + 1) % 2)

    @pl.when((i == 0) | (te_ref[i] != te_ref[jnp.maximum(i - 1, 0)]))
    def _():
        wgb_ref[...] = wg_ref[0, 0].astype(BF16)
        wub_ref[...] = wu_ref[0, 0].astype(BF16)
        wdb_ref[...] = wd_ref[0, 0].astype(BF16)

    @pl.when(tv_ref[i] > 0)
    def _():
        slot = i % 2
        pltpu.make_async_copy(x_hbm.at[pl.ds(0, tm)], xbuf_ref.at[slot], sem_ref.at[slot]).wait()
        xs = xbuf_ref[slot].astype(BF16)
        hg = _dot(xs, wgb_ref[...])
        hu = _dot(xs, wub_ref[...])
        act = (hg * jax.nn.sigmoid(hg)) * hu * gate_ref[...]
        o_ref[...] = _dot(act.astype(BF16), wdb_ref[...]).astype(o_ref.dtype)

    @pl.when(tv_ref[i] == 0)
    def _():
        o_ref[...] = jnp.zeros_like(o_ref)


def expert_ffn(x, tok_sorted, gate_sorted, tile_expert, tile_valid, w_gate, w_up, w_down, layer, tm):
    d = x.shape[1]
    p = tok_sorted.shape[0]
    f = w_gate.shape[-1]
    ntiles = p // tm
    grid_spec = pltpu.PrefetchScalarGridSpec(
        num_scalar_prefetch=3,
        grid=(ntiles,),
        in_specs=[pl.BlockSpec(memory_space=pl.ANY),
                  pl.BlockSpec((tm, 1), lambda i, te, tv, tok: (i, 0)),
                  pl.BlockSpec((1, 1, d, f), lambda i, te, tv, tok: (layer, te[i], 0, 0)),
                  pl.BlockSpec((1, 1, d, f), lambda i, te, tv, tok: (layer, te[i], 0, 0)),
                  pl.BlockSpec((1, 1, f, d), lambda i, te, tv, tok: (layer, te[i], 0, 0))],
        out_specs=pl.BlockSpec((tm, d), lambda i, te, tv, tok: (i, 0)),
        scratch_shapes=[pltpu.VMEM((d, f), BF16), pltpu.VMEM((d, f), BF16), pltpu.VMEM((f, d), BF16),
                        pltpu.VMEM((2, tm, d), F32), pltpu.SemaphoreType.DMA((2,))],
    )
    return pl.pallas_call(
        _expert_body,
        grid_spec=grid_spec,
        out_shape=jax.ShapeDtypeStruct((p, d), BF16),
        compiler_params=_cparams(("arbitrary",)),
        name="expert_ffn",
    )(tile_expert, tile_valid, tok_sorted, x, gate_sorted, w_gate, w_up, w_down)


MOE_TILE = 256


def hier_moe(x_f32, routing, w_gate, w_up, w_down, layer):
    n, d = x_f32.shape
    tm = MOE_TILE
    eid, gates, rank, counts = routing
    padded = ((counts + tm - 1) // tm) * tm
    seg_end = jnp.cumsum(padded)
    seg_start = seg_end - padded
    experts = jnp.arange(N_EXPERTS, dtype=jnp.int32)
    pos = (jnp.sum(jnp.where(eid[..., None] == experts, seg_start, 0), axis=-1) + rank).reshape(-1)
    p = 2 * n + N_EXPERTS * tm
    tok = jnp.arange(2 * n, dtype=jnp.int32) // 2
    packed = jnp.stack([tok, lax.bitcast_convert_type(gates.reshape(-1), jnp.int32)], axis=-1)
    packed = jnp.zeros((p, 2), jnp.int32).at[pos].set(packed)
    tok_sorted = packed[:, 0]
    gate_sorted = lax.bitcast_convert_type(packed[:, 1], F32)
    tile_start = jnp.arange(p // tm, dtype=jnp.int32) * tm
    tile_valid = (tile_start < seg_end[-1]).astype(jnp.int32)
    tile_expert = jnp.sum((tile_start[:, None] >= seg_end[None, :]).astype(jnp.int32), axis=1)
    tile_expert = jnp.minimum(tile_expert, N_EXPERTS - 1)
    last_e = jnp.max(jnp.where(tile_valid > 0, tile_expert, 0))
    tile_expert = jnp.where(tile_valid > 0, tile_expert, last_e).astype(jnp.int32)
    ys = expert_ffn(x_f32, tok_sorted, gate_sorted.reshape(p, 1), tile_expert, tile_valid,
                    w_gate, w_up, w_down, layer, tm)
    pos2 = pos.reshape(n, 2)
    return (ys.at[pos2[:, 0]].get(mode="promise_in_bounds"),
            ys.at[pos2[:, 1]].get(mode="promise_in_bounds"))


C_HEAD = 128
HGRN_LEVELS = (1, 2, 4, 8, 16, 32)


def _cumsum_rows(x, row):
    axis = x.ndim - 2
    s = 1
    while s < x.shape[axis]:
        x = x + jnp.where(row >= s, pltpu.roll(x, s, axis=axis), 0.0)
        s *= 2
    return x


def _anchor_rows(b, m):
    n = b.shape[0]
    if m >= 8:
        parts = [jnp.broadcast_to(b[base + m:base + m + 1, :], (2 * m, b.shape[1]))
                 for base in range(0, n, 2 * m)]
        return parts[0] if len(parts) == 1 else jnp.concatenate(parts, axis=0)
    b3 = b.reshape(n // 8, 8, b.shape[1])
    sub = lax.broadcasted_iota(jnp.int32, b3.shape, 1)
    out = None
    for base in range(8 - 2 * m, -1, -2 * m):
        mid = jnp.broadcast_to(b3[:, base + m:base + m + 1, :], b3.shape)
        out = mid if out is None else jnp.where(sub < base + 2 * m, mid, out)
    return out.reshape(n, b.shape[1])


def _hgrn2_body(q_ref, f_ref, i_ref, g_ref, loglb_ref, log1mlb_ref, omlb_ref, gain_ref,
                o_ref, st_ref, *, hb, n_chunk):
    @pl.when(pl.program_id(2) == 0)
    def _():
        st_ref[...] = jnp.zeros_like(st_ref)

    row = lax.broadcasted_iota(jnp.int32, (CHUNK, hb * C_HEAD), 0)
    tt = lax.broadcasted_iota(jnp.int32, (CHUNK, CHUNK), 0)
    ss = lax.broadcasted_iota(jnp.int32, (CHUNK, CHUNK), 1)
    txs = tt ^ ss
    lvl_id = jnp.where(tt == ss, 0, -1)
    for li, m in enumerate(HGRN_LEVELS):
        lvl_id = jnp.where((ss < tt) & (txs >= m) & (txs < 2 * m), li + 1, lvl_id)

    heads = [slice(h * C_HEAD, (h + 1) * C_HEAD) for h in range(hb)]

    def head_scores(qm, km):
        return jnp.stack([_dot_nt(qm[:, hs], km[:, hs]) for hs in heads])

    def chunk_step(c, carry):
        rs = pl.ds(pl.multiple_of(c * CHUNK, CHUNK), CHUNK)
        q = q_ref[rs, :]
        f = f_ref[rs, :]
        v = i_ref[rs, :].astype(BF16)
        g = g_ref[rs, :]
        qs = q * jax.nn.sigmoid(q)
        log_sig = jnp.minimum(f, 0.0) - jnp.log1p(jnp.exp(-jnp.abs(f)))
        y = log1mlb_ref[...] + log_sig
        x = loglb_ref[...]
        log_f = jnp.maximum(x, y) + jnp.log1p(jnp.exp(-jnp.abs(x - y)))
        key = omlb_ref[...] * jax.nn.sigmoid(-f)
        b = _cumsum_rows(log_f, row)
        scores = jnp.where(lvl_id == 0, head_scores(qs.astype(BF16), key.astype(BF16)), 0.0)
        for li, m in enumerate(HGRN_LEVELS):
            upper = (row & m) != 0
            e = jnp.exp(jnp.where(upper, 1.0, -1.0) * (b - _anchor_rows(b, m)))
            xm = (jnp.where(upper, qs, key) * e).astype(BF16)
            scores = jnp.where(lvl_id == li + 1, head_scores(xm, xm), scores)
        scores = scores.astype(BF16)
        q_dec = (qs * jnp.exp(b)).astype(BF16)
        b_last = b[CHUNK - 1:CHUNK, :]
        k_dec = (key * jnp.exp(b_last - b)).astype(BF16)
        s_dec = jnp.exp(b_last)
        outs = []
        for h, hs in enumerate(heads):
            st = st_ref[h]
            o = _dot(scores[h], v[:, hs]) + _dot_nt(q_dec[:, hs], st.astype(BF16))
            st_ref[h] = st * s_dec[:, hs] + _dot_tn(v[:, hs], k_dec[:, hs])
            outs.append(o * lax.rsqrt(jnp.mean(o * o, axis=-1, keepdims=True) + RMS_EPS))
        o = jnp.concatenate(outs, axis=1) * gain_ref[...] * (g * jax.nn.sigmoid(g))
        o_ref[rs, :] = o.astype(o_ref.dtype)
        return carry

    lax.fori_loop(0, n_chunk, chunk_step, 0)


def hgrn2(cols, lb, norm_gain, bsz, seq, hb=8, tc=512):
    n, c4 = cols.shape
    c = c4 // 4
    wl = hb * C_HEAD
    nhb = c // wl
    nt = seq // tc
    lb = lb.astype(F32).reshape(1, c)
    loglb = jnp.log(lb)
    log1mlb = jnp.log1p(-lb)
    omlb = 1.0 - lb

    def col(k):
        return pl.BlockSpec((tc, wl), lambda b, h, t, k=k: (b * nt + t, k * nhb + h))

    vec = pl.BlockSpec((1, wl), lambda b, h, t: (0, h))
    return pl.pallas_call(
        functools.partial(_hgrn2_body, hb=hb, n_chunk=tc // CHUNK),
        grid=(bsz, nhb, nt),
        in_specs=[col(0), col(1), col(2), col(3), vec, vec, vec, vec],
        out_specs=pl.BlockSpec((tc, wl), lambda b, h, t: (b * nt + t, h)),
        out_shape=jax.ShapeDtypeStruct((n, c), BF16),
        scratch_shapes=[pltpu.VMEM((hb, C_HEAD, C_HEAD), F32)],
        compiler_params=_cparams(("parallel", "parallel", "arbitrary")),
        name="hgrn2",
    )(cols, cols, cols, cols, loglb, log1mlb, omlb, norm_gain.reshape(1, c))


B_WIDTH = 2048
B_HEAD = 64
B_HEADS = B_WIDTH // B_HEAD
B_LORA_PAD = 128
B_GATE_LORA = 256
B_COLS_PAD = 3 * B_WIDTH + 2 * B_LORA_PAD + B_GATE_LORA


def _head_pair_ones():
    r = lax.broadcasted_iota(jnp.int32, (LANES, LANES), 0) // B_HEAD
    c = lax.broadcasted_iota(jnp.int32, (LANES, LANES), 1) // B_HEAD
    return (r == c).astype(BF16)


def _head_sums(x, ones):
    outs = []
    for j in range(x.shape[1] // LANES):
        hi, lo = _split_bf16(x[:, j * LANES:(j + 1) * LANES])
        outs.append(_dot(hi, ones) + _dot(lo, ones))
    return outs[0] if len(outs) == 1 else jnp.concatenate(outs, axis=1)


def _dot3(a, wh, wl):
    ah, al = _split_bf16(a)
    return _dot(ah, wh) + (_dot(al, wh) + _dot(ah, wl))


def _softplus(z):
    return jnp.maximum(z, 0.0) + jnp.log1p(jnp.exp(-jnp.abs(z)))


def _rwkv_prep_body(c_ref, p_ref, mu_ref, w0_ref, a0_ref, kk_ref, ka_ref, rk_ref,
                    wuh_ref, wul_ref, auh_ref, aul_ref, gu_ref,
                    r_o, lw_o, k_o, v_o, kk_o, a_o, g_o, bonus_o, *, tiles_per_seq):
    i = pl.program_id(0)
    x = c_ref[...]
    tm = x.shape[0]
    row = lax.broadcasted_iota(jnp.int32, x.shape, 0)
    prev = jnp.where(i % tiles_per_seq == 0, 0.0, p_ref[7:8, :])
    shifted = jnp.where(row == 0, prev, pltpu.roll(x, 1, axis=0))
    mixed = x + (shifted - x) * mu_ref[...]
    w = B_WIDTH
    r = mixed[:, 0:w]
    k = mixed[:, w:2 * w]
    v = mixed[:, 2 * w:3 * w]
    wd = mixed[:, 3 * w:3 * w + B_LORA_PAD]
    ad = mixed[:, 3 * w + B_LORA_PAD:3 * w + 2 * B_LORA_PAD]
    gd = mixed[:, 3 * w + 2 * B_LORA_PAD:]
    wlog = -_softplus(-(w0_ref[...] + _dot3(jnp.tanh(wd), wuh_ref[...], wul_ref[...]))) - 0.5
    a = jax.nn.sigmoid(a0_ref[...] + _dot3(ad, auh_ref[...], aul_ref[...]))
    g = _dot(jax.nn.sigmoid(gd).astype(BF16), gu_ref[...])
    kk = k * kk_ref[...]
    ssq = _head_sums(kk * kk, _head_pair_ones())
    kk = kk / jnp.maximum(jnp.sqrt(ssq), 1e-12)
    k_mod = k * (1.0 + (a - 1.0) * ka_ref[...])
    g_o[...] = g
    bonus_o[...] = _head_sums(r * k_mod * rk_ref[...], _head_pair_ones()) * v
    for out_ref, val in ((r_o, r), (lw_o, -jnp.exp(wlog)), (k_o, k_mod), (v_o, v), (kk_o, kk), (a_o, a)):
        for h in range(B_HEADS):
            out_ref[h] = val[:, h * B_HEAD:(h + 1) * B_HEAD]


def _pad_rows(w, rows):
    return jnp.zeros((rows, w.shape[1]), w.dtype).at[:w.shape[0]].set(w)


def rwkv_prep(cols, mu_pad, w0, w_up, a0, a_up, g_up, k_k, k_a, r_k, seq, tm=128):
    n, cw = cols.shape
    w = B_WIDTH
    wuh, wul = _split_bf16(_pad_rows(w_up, B_LORA_PAD))
    auh, aul = _split_bf16(_pad_rows(a_up, B_LORA_PAD))
    row = pl.BlockSpec((tm, w), lambda i: (i, 0))
    heads = pl.BlockSpec((B_HEADS, tm, B_HEAD), lambda i: (0, i, 0))
    vec = pl.BlockSpec((1, w), lambda i: (0, 0))
    lora = pl.BlockSpec((B_LORA_PAD, w), lambda i: (0, 0))
    outs = pl.pallas_call(
        functools.partial(_rwkv_prep_body, tiles_per_seq=seq // tm),
        grid=(n // tm,),
        in_specs=[pl.BlockSpec((tm, cw), lambda i: (i, 0)),
                  pl.BlockSpec((8, cw), lambda i: (jnp.maximum(i * (tm // 8) - 1, 0), 0)),
                  pl.BlockSpec((1, cw), lambda i: (0, 0)),
                  vec, vec, vec, vec, vec, lora, lora, lora, lora,
                  pl.BlockSpec((B_GATE_LORA, w), lambda i: (0, 0))],
        out_specs=[heads] * 6 + [row] * 2,
        out_shape=[jax.ShapeDtypeStruct((B_HEADS, n, B_HEAD), F32)] * 6 + [jax.ShapeDtypeStruct((n, w), F32)] * 2,
        compiler_params=_cparams(("parallel",)),
        name="rwkv_prep",
    )(cols, cols, mu_pad.reshape(1, cw), w0.reshape(1, w), a0.reshape(1, w), k_k.reshape(1, w),
      k_a.reshape(1, w), r_k.reshape(1, w), wuh, wul, auh, aul, g_up.astype(BF16))
    return outs


def _rwkv_scan_body(r_ref, lw_ref, k_ref, v_ref, kk_ref, a_ref, y_ref, st_ref, qm_ref, ys_ref, *, hb, n_chunk):
    @pl.when(pl.program_id(2) == 0)
    def _():
        st_ref[...] = jnp.zeros_like(st_ref)

    L = CHUNK
    row = lax.broadcasted_iota(jnp.int32, (hb, L, B_HEAD), 1)
    tt = lax.broadcasted_iota(jnp.int32, (L, L), 0)
    ss = lax.broadcasted_iota(jnp.int32, (L, L), 1)
    strict = ss < tt
    incl = ss <= tt
    eye = ss == tt
    eye_f = eye.astype(F32)

    def local_step(c, carry):
        sl = pl.ds(pl.multiple_of(c * L, L), L)
        r = r_ref[:, sl, :]
        lw = lw_ref[:, sl, :]
        k = k_ref[:, sl, :]
        v = v_ref[:, sl, :]
        kk = kk_ref[:, sl, :]
        a = a_ref[:, sl, :]
        lp = _cumsum_rows(lw, row)
        p = jnp.exp(lp)
        inv_p = jnp.exp(-lp)
        at = -kk * jnp.exp(lp - lw)
        bt = kk * a * inv_p
        kt = k * inv_p
        rt = r * p
        p_last = p[:, L - 1:L, :]
        bp = bt * p_last
        kp = kt * p_last
        ar = jnp.concatenate([at, rt], axis=1)
        m_b = _bdot_nt(ar, bt)
        m_k = _bdot_nt(ar, kt)
        nmat = jnp.where(strict, m_b[:, :L], 0.0)
        a_rb = jnp.where(incl, m_b[:, L:], 0.0)
        g = _bdot(jnp.where(strict, m_k[:, :L], 0.0), v)
        tmat = eye_f + nmat
        npow = nmat
        lvl = 2
        while lvl < L:
            npow = _bdot(npow, npow)
            tmat = tmat + _bdot(npow, tmat)
            lvl *= 2
        w = _bdot(tmat, at)
        u = _bdot(tmat, g)
        qe = rt + _bdot(a_rb, w)
        yl = _bdot(a_rb, u) + _bdot(jnp.where(incl, m_k[:, L:], 0.0), v)
        qm_ref[:, c, 0:L, :] = qe
        ys_ref[:, c, 0:L, :] = yl
        for h in range(hb):
            qm_ref[h, c, L:2 * L, :] = _dot_tn(bp[h], w[h]) + jnp.where(eye, p_last[h], 0.0)
            ys_ref[h, c, L:2 * L, :] = _dot_tn(bp[h], u[h]) + _dot_tn(kp[h], v[h])
        return carry

    lax.fori_loop(0, n_chunk, local_step, 0)

    def serial_step(c, carry):
        sl = pl.ds(pl.multiple_of(c * L, L), L)
        out = _bdot(qm_ref[:, c], st_ref[...]) + ys_ref[:, c]
        y_ref[:, sl, :] = out[:, :L]
        st_ref[...] = out[:, L:]
        return carry

    lax.fori_loop(0, n_chunk, serial_step, 0)


def rwkv_scan(r, lw, k, v, kk, a, bsz, seq, hb=16, tc=256):
    nh, n, hd = r.shape
    tc = min(tc, seq)
    nt = seq // tc
    n_chunk = tc // CHUNK
    blk = pl.BlockSpec((hb, tc, hd), lambda b, h, t: (h, b * nt + t, 0))
    return pl.pallas_call(
        functools.partial(_rwkv_scan_body, hb=hb, n_chunk=n_chunk),
        grid=(bsz, nh // hb, nt),
        in_specs=[blk] * 6,
        out_specs=blk,
        out_shape=jax.ShapeDtypeStruct((nh, n, hd), F32),
        scratch_shapes=[pltpu.VMEM((hb, hd, hd), F32),
                        pltpu.VMEM((hb, n_chunk, 2 * CHUNK, hd), F32),
                        pltpu.VMEM((hb, n_chunk, 2 * CHUNK, hd), F32)],
        compiler_params=_cparams(("parallel", "parallel", "arbitrary")),
        name="rwkv_scan",
    )(r, lw, k, v, kk, a)


def _rwkv_post_body(y_ref, g_ref, bonus_ref, lg_ref, lb_ref, o_ref):
    ones = _head_pair_ones()
    y = jnp.concatenate([y_ref[h] for h in range(B_HEADS)], axis=1)
    mean = _head_sums(y, ones) * (1.0 / B_HEAD)
    yc = y - mean
    var = _head_sums(yc * yc, ones) * (1.0 / B_HEAD)
    yn = yc * lax.rsqrt(var + GN_EPS) * lg_ref[...] + lb_ref[...]
    o_ref[...] = ((yn + bonus_ref[...]) * g_ref[...]).astype(o_ref.dtype)


def rwkv_post(y_heads, g, bonus, ln_gain, ln_bias, tm=256):
    n, w = g.shape
    row = pl.BlockSpec((tm, w), lambda i: (i, 0))
    vec = pl.BlockSpec((1, w), lambda i: (0, 0))
    return pl.pallas_call(
        _rwkv_post_body,
        grid=(n // tm,),
        in_specs=[pl.BlockSpec((B_HEADS, tm, B_HEAD), lambda i: (0, i, 0)), row, row, vec, vec],
        out_specs=row,
        out_shape=jax.ShapeDtypeStruct((n, w), BF16),
        compiler_params=_cparams(("parallel",)),
        name="rwkv_post",
    )(y_heads, g, bonus, ln_gain.reshape(1, w), ln_bias.reshape(1, w))


def mixer_rwkv7(cols_b, mu_pad, w0, w_up, a0, a_up, g_up, k_k, k_a, r_k, ln_gain, ln_bias, bsz, seq):
    r, lw, k, v, kk, a, g, bonus = rwkv_prep(cols_b, mu_pad, w0, w_up, a0, a_up, g_up, k_k, k_a,
                                             r_k.reshape(-1), seq)
    y_heads = rwkv_scan(r, lw, k, v, kk, a, bsz, seq)
    return rwkv_post(y_heads, g, bonus, ln_gain, ln_bias)


A_Q_RANK = 768
A_KV_RANK = 512
IDX_DIM = 64
IDX_HEADS = 16
A_HEADS = 16
A_HEAD_DIM = 128
A_COLS_PAD = A_Q_RANK + A_KV_RANK + 2 * LANES
INT_MIN = -2 ** 31


def _dsa_prep_body(c_ref, qn_ref, kn_ref, g_ref, b_ref, q_o, c_o, k_o, w_o):
    x = c_ref[...]
    ql = x[:, :A_Q_RANK]
    q_o[...] = (ql * lax.rsqrt(jnp.mean(ql * ql, axis=-1, keepdims=True) + RMS_EPS) * qn_ref[...]).astype(q_o.dtype)
    kv = x[:, A_Q_RANK:A_Q_RANK + A_KV_RANK]
    c_o[...] = (kv * lax.rsqrt(jnp.mean(kv * kv, axis=-1, keepdims=True) + RMS_EPS) * kn_ref[...]).astype(c_o.dtype)
    o = A_Q_RANK + A_KV_RANK
    ki = x[:, o:o + LANES]
    valid = lax.broadcasted_iota(jnp.int32, ki.shape, 1) < IDX_DIM
    mu = jnp.sum(jnp.where(valid, ki, 0.0), axis=-1, keepdims=True) * (1.0 / IDX_DIM)
    kc = jnp.where(valid, ki - mu, 0.0)
    var = jnp.sum(kc * kc, axis=-1, keepdims=True) * (1.0 / IDX_DIM)
    kn = kc * lax.rsqrt(var + LN_EPS) * g_ref[...] + b_ref[...]
    k_o[...] = kn[:, :IDX_DIM].astype(k_o.dtype)
    w_o[...] = x[:, o + LANES:o + 2 * LANES] * (IDX_HEADS ** -0.5 * IDX_DIM ** -0.5)


def dsa_prep(cols, q_norm, kv_norm, k_gain, k_bias, tm=512):
    n, cw = cols.shape
    pad = jnp.zeros((LANES - IDX_DIM,), F32)
    return pl.pallas_call(
        _dsa_prep_body,
        grid=(n // tm,),
        in_specs=[pl.BlockSpec((tm, cw), lambda i: (i, 0)),
                  pl.BlockSpec((1, A_Q_RANK), lambda i: (0, 0)),
                  pl.BlockSpec((1, A_KV_RANK), lambda i: (0, 0)),
                  pl.BlockSpec((1, LANES), lambda i: (0, 0)),
                  pl.BlockSpec((1, LANES), lambda i: (0, 0))],
        out_specs=[pl.BlockSpec((tm, A_Q_RANK), lambda i: (i, 0)),
                   pl.BlockSpec((tm, A_KV_RANK), lambda i: (i, 0)),
                   pl.BlockSpec((tm, IDX_DIM), lambda i: (i, 0)),
                   pl.BlockSpec((tm, LANES), lambda i: (i, 0))],
        out_shape=[jax.ShapeDtypeStruct((n, A_Q_RANK), BF16), jax.ShapeDtypeStruct((n, A_KV_RANK), BF16),
                   jax.ShapeDtypeStruct((n, IDX_DIM), BF16), jax.ShapeDtypeStruct((n, LANES), F32)],
        compiler_params=_cparams(("parallel",)),
        name="dsa_prep",
    )(cols, q_norm.reshape(1, -1), kv_norm.reshape(1, -1),
      jnp.concatenate([k_gain, pad]).reshape(1, LANES), jnp.concatenate([k_bias, pad]).reshape(1, LANES))


def _head_mm_body(a_ref, w_ref, o_ref, *, scale):
    o_ref[0] = (_dot(a_ref[...], w_ref[0]) * scale).astype(o_ref.dtype)


def head_matmul(a, w, scale, tm=1024):
    n = a.shape[0]
    nh, k, f = w.shape
    tm = min(tm, n)
    return pl.pallas_call(
        functools.partial(_head_mm_body, scale=scale),
        grid=(nh, n // tm),
        in_specs=[pl.BlockSpec((tm, k), lambda h, i: (i, h)),
                  pl.BlockSpec((1, k, f), lambda h, i: (h, 0, 0))],
        out_specs=pl.BlockSpec((1, tm, f), lambda h, i: (h, i, 0)),
        out_shape=jax.ShapeDtypeStruct((nh, n, f), BF16),
        compiler_params=_cparams(("parallel", "parallel")),
        name="head_matmul",
    )(a, w)


DSA_KT = 256
DSA_AT = 1024
DSA_HEAD_GROUP = 4


def _dsa_attend_body(qi_ref, wi_ref, qa_ref, ck_ref, kx_ref, wuv_ref, sl_ref, o_ref,
                     keys_ref, m_ref, l_ref, acc_ref, *, top_k, seq, at_tile):
    j = pl.program_id(1)
    at_n = ((j + 1) * Q_BLOCK + at_tile - 1) // at_tile
    kt_n = at_n * (at_tile // DSA_KT)
    nq = Q_BLOCK
    nh = A_HEADS
    qrow = lax.broadcasted_iota(jnp.int32, (nq, 1), 0)
    qpos = j * Q_BLOCK + qrow
    limit = (qpos // CHUNK + 1) * CHUNK
    lane = lax.broadcasted_iota(jnp.int32, (nq, DSA_KT), 1)
    lane_a = lax.broadcasted_iota(jnp.int32, (nq, at_tile), 1)

    qi = qi_ref[...].reshape(nh * nq, IDX_DIM)
    wi = wi_ref[...]

    def score_tile(kt, carry):
        starts = [pl.multiple_of(kt * at_tile + t * DSA_KT, DSA_KT) for t in range(at_tile // DSA_KT)]
        rels = [_dot_nt(qi, kx_ref[pl.ds(k0, DSA_KT), :]) for k0 in starts]
        for k0, rel in zip(starts, rels):
            score = jnp.sum(jnp.maximum(rel, 0.0).reshape(nh, nq, DSA_KT) * wi, axis=0)
            bits = lax.bitcast_convert_type(score, jnp.int32)
            key = bits ^ ((bits >> 31) & 0x7FFFFFFF)
            keys_ref[:, pl.ds(k0, DSA_KT)] = jnp.where(k0 + lane < limit, key, INT_MIN)
        return carry

    lax.fori_loop(0, at_n, score_tile, 0)

    def count_keys(pred):
        def count_tile(kt, cnt):
            k0 = pl.multiple_of(kt * at_tile, at_tile)
            hit = pred(keys_ref[:, pl.ds(k0, at_tile)], k0).astype(jnp.int32)
            for t in range(at_tile // LANES):
                cnt = cnt + hit[:, t * LANES:(t + 1) * LANES]
            return cnt

        cnt = lax.fori_loop(0, at_n, count_tile, jnp.zeros((nq, LANES), jnp.int32))
        return jnp.sum(cnt, axis=-1, keepdims=True)

    def bit_step(i, carry):
        ans, n_ge = carry
        cand = ans | lax.shift_left(jnp.int32(1), 31 - i)
        cand_s = cand ^ INT_MIN
        total = count_keys(lambda key, k0: key >= cand_s)
        ok = total >= top_k
        return jnp.where(ok, cand, ans), jnp.where(ok, total, n_ge)

    zero = jnp.zeros((nq, 1), jnp.int32)
    ans, n_ge = lax.fori_loop(0, 32, bit_step, (zero, zero))
    thr = ans ^ INT_MIN

    excess = (ans != 0) & (n_ge > top_k)
    idx_bits = max(1, (seq - 1).bit_length())

    def tie_cut():
        need = top_k - count_keys(lambda key, k0: key > thr)

        def idx_step(i, cut):
            cand = cut | lax.shift_left(jnp.int32(1), idx_bits - 1 - i)
            below = count_keys(lambda key, k0: (key == thr) & (k0 + lane_a < cand))
            return jnp.where(below < need, cand, cut)

        return lax.fori_loop(0, idx_bits, idx_step, zero)

    cut = lax.cond(jnp.max(excess.astype(jnp.int32)) > 0, tie_cut, lambda: zero)
    cut = jnp.where(excess, cut, seq)

    m_ref[...] = jnp.full_like(m_ref, -jnp.inf)
    l_ref[...] = jnp.zeros_like(l_ref)
    acc_ref[...] = jnp.zeros_like(acc_ref)
    slopes = sl_ref[...]
    gh = DSA_HEAD_GROUP

    def attend_tile(kt, carry):
        k0 = pl.multiple_of(kt * at_tile, at_tile)
        ck = ck_ref[pl.ds(k0, at_tile), :]
        key = keys_ref[:, pl.ds(k0, at_tile)]
        kidx = k0 + lane_a
        keep = ((key > thr) | ((key == thr) & (kidx <= cut))) & (key > INT_MIN)
        neg_dist = jnp.where(keep, -jnp.abs(qpos - kidx).astype(F32), -jnp.inf)
        def qk(g):
            return _dot_nt(qa_ref[g * gh:(g + 1) * gh].reshape(gh * nq, A_KV_RANK), ck)

        s_next = qk(0)
        for g in range(nh // gh):
            s = s_next
            if g + 1 < nh // gh:
                s_next = qk(g + 1)
            rows = pl.ds(g * gh * nq, gh * nq)
            s = s.reshape(gh, nq, at_tile)
            s = (s + slopes[g * gh:(g + 1) * gh] * neg_dist).reshape(gh * nq, at_tile)
            m_old = m_ref[rows, :]
            m_new = jnp.maximum(m_old, jnp.max(s, axis=-1, keepdims=True))
            m_safe = jnp.where(m_new == -jnp.inf, 0.0, m_new)
            p = jnp.exp(s - m_safe)
            alpha = jnp.exp(m_old - m_safe)
            l_ref[rows, :] = alpha * l_ref[rows, :] + jnp.sum(p, axis=-1, keepdims=True)
            acc_ref[rows, :] = alpha * acc_ref[rows, :] + _dot(p.astype(BF16), ck)
            m_ref[rows, :] = m_new
        return carry

    lax.fori_loop(0, at_n, attend_tile, 0)
    o_lat = (acc_ref[...] / l_ref[...]).astype(BF16).reshape(nh, nq, A_KV_RANK)
    for h in range(nh):
        o_ref[:, h * A_HEAD_DIM:(h + 1) * A_HEAD_DIM] = _dot(o_lat[h], wuv_ref[h]).astype(o_ref.dtype)


def dsa_attend(qi, wi, qa, ckv, kx, w_uv, bsz, seq):
    n = ckv.shape[0]
    nblk = seq // Q_BLOCK
    top_k = min(TOPK_MAX, seq // 4)
    start = 2.0 ** (-8.0 / A_HEADS)
    slopes = jnp.asarray([start ** (h + 1) for h in range(A_HEADS)], F32).reshape(A_HEADS, 1, 1)

    def qblk(last):
        return pl.BlockSpec((A_HEADS, Q_BLOCK, last), lambda b, j: (0, b * nblk + j, 0))

    return pl.pallas_call(
        functools.partial(_dsa_attend_body, top_k=top_k, seq=seq, at_tile=min(DSA_AT, seq)),
        grid=(bsz, nblk),
        in_specs=[qblk(IDX_DIM), qblk(1), qblk(A_KV_RANK),
                  pl.BlockSpec((seq, A_KV_RANK), lambda b, j: (b, 0), pipeline_mode=pl.Buffered(1)),
                  pl.BlockSpec((seq, IDX_DIM), lambda b, j: (b, 0)),
                  pl.BlockSpec((A_HEADS, A_KV_RANK, A_HEAD_DIM), lambda b, j: (0, 0, 0)),
                  pl.BlockSpec((A_HEADS, 1, 1), lambda b, j: (0, 0, 0))],
        out_specs=pl.BlockSpec((Q_BLOCK, A_HEADS * A_HEAD_DIM), lambda b, j: (b * nblk + j, 0)),
        out_shape=jax.ShapeDtypeStruct((n, A_HEADS * A_HEAD_DIM), BF16),
        scratch_shapes=[pltpu.VMEM((Q_BLOCK, seq), jnp.int32),
                        pltpu.VMEM((A_HEADS * Q_BLOCK, 1), F32),
                        pltpu.VMEM((A_HEADS * Q_BLOCK, 1), F32),
                        pltpu.VMEM((A_HEADS * Q_BLOCK, A_KV_RANK), F32)],
        compiler_params=_cparams(("parallel", "arbitrary")),
        name="dsa_attend",
    )(qi, wi, qa, ckv, kx, w_uv, slopes)


def mixer_dsa(cols_a, q_norm, kv_norm, w_uq, w_uk, w_uv, w_idx_q, k_gain, k_bias, bsz, seq):
    n = cols_a.shape[0]
    q_lat, c_kv, k_idx, w_idx = dsa_prep(cols_a, q_norm, kv_norm, k_gain, k_bias)
    q = matmul(q_lat, w_uq.reshape(A_Q_RANK, -1).astype(BF16), out_dtype=BF16)
    qa = head_matmul(q, w_uk.transpose(1, 2, 0).astype(BF16), A_HEAD_DIM ** -0.5)
    qi = matmul(q_lat, w_idx_q.reshape(A_Q_RANK, -1).astype(BF16), out_dtype=BF16)
    qi = qi.reshape(n, IDX_HEADS, IDX_DIM).transpose(1, 0, 2)
    wi = w_idx[:, :IDX_HEADS].T.reshape(IDX_HEADS, n, 1)
    return dsa_attend(qi, wi, qa, c_kv, k_idx, w_uv.transpose(1, 0, 2).astype(BF16), bsz, seq)


A_SPLITS = (A_Q_RANK, A_KV_RANK, IDX_DIM, IDX_HEADS)
A_COLS = sum(A_SPLITS)
DECAY_LORA = 96
AAA_LORA = 96


def _pad_cols(w, sizes, padded):
    out, o = [], 0
    for s, p in zip(sizes, padded):
        piece = w[..., o:o + s]
        if p > s:
            piece = jnp.concatenate([piece, jnp.zeros(w.shape[:-1] + (p - s,), w.dtype)], axis=-1)
        out.append(piece)
        o += s
    return jnp.concatenate(out, axis=-1)


def kernel(x, ln1_gain, ln1_bias, ln2_gain, ln2_bias, w_in_even, a_q_norm, a_kv_norm, a_w_uq, a_w_uk, a_w_uv, a_w_idx_q, a_idx_k_gain, a_idx_k_bias, b_mu, b_w0, b_w_up, b_a0, b_a_up, b_g_up, b_k_k, b_k_a, b_r_k, b_ln_gain, b_ln_bias, w_out_even, w_in_odd, c_lb_logits, c_norm_gain, w_out_odd, router_group, router_group_bias, router_expert, router_expert_bias, moe_w_gate, moe_w_up, moe_w_down):
    bsz, seq, d = x.shape
    n = bsz * seq
    lb_table = jnp.cumsum(jax.nn.softmax(c_lb_logits.astype(F32), axis=0), axis=0)
    lb_table = lb_table - lb_table[:1]
    a_sizes, a_padded = A_SPLITS, (A_Q_RANK, A_KV_RANK, LANES, LANES)
    b_sizes = (3 * B_WIDTH, DECAY_LORA, AAA_LORA, B_GATE_LORA)
    b_padded = (3 * B_WIDTH, B_LORA_PAD, B_LORA_PAD, B_GATE_LORA)
    xf = x.reshape(n, d)
    xb = xf.astype(BF16)
    for layer in range(DEPTH):
        j = layer // 2
        if layer % 2 == 0:
            w_in = w_in_even[j]
            w_a = _pad_cols(w_in[:, :A_COLS], a_sizes, a_padded).astype(BF16)
            w_b = _pad_cols(w_in[:, A_COLS:], b_sizes, b_padded).astype(BF16)
            cols_a = matmul(xb, w_a, tn=512)
            cols_b = matmul(xb, w_b, tn=512)
            y_a = mixer_dsa(cols_a, a_q_norm[j], a_kv_norm[j], a_w_uq[j], a_w_uk[j], a_w_uv[j],
                            a_w_idx_q[j], a_idx_k_gain[j], a_idx_k_bias[j], bsz, seq)
            y_b = mixer_rwkv7(cols_b, _pad_cols(b_mu[j], b_sizes, b_padded), b_w0[j], b_w_up[j], b_a0[j],
                              b_a_up[j], b_g_up[j], b_k_k[j], b_k_a[j], b_r_k[j], b_ln_gain[j], b_ln_bias[j],
                              bsz, seq)
            mixed = jnp.concatenate([y_a, y_b], axis=-1)
            w_out = w_out_even[j]
        else:
            cols = matmul_stacked_w(xb, w_in_odd, j)
            mixed = hgrn2(cols, lb_table[j], c_norm_gain[j], bsz, seq)
            w_out = w_out_odd[j]
        mix = matmul(mixed, w_out.astype(BF16))
        xf, xb, routing = residual_ln_route(xf, mix, ln1_gain[layer], ln1_bias[layer],
                                            router_group[layer], router_group_bias[layer],
                                            router_expert[layer], router_expert_bias[layer])
        y0, y1 = hier_moe(xf, routing, moe_w_gate, moe_w_up, moe_w_down, layer)
        xf, xb = add_ln(xf, y0, y1, ln2_gain[layer], ln2_bias[layer])
    return xf.reshape(bsz, seq, d)
```

```python
import functools

import jax
import jax.numpy as jnp
from jax import lax
from jax.experimental import pallas as pl
from jax.experimental.pallas import tpu as pltpu

F32 = jnp.float32
BF16 = jnp.bfloat16

DEPTH = 4
ALPHA = (2 * DEPTH) ** 0.25
LN_EPS = 1e-5
RMS_EPS = 1e-6
GN_EPS = 64e-5
CHUNK = 64
Q_BLOCK = 128
TOPK_MAX = 256
N_GROUPS = 4
EXPERTS_PER_GROUP = 8
N_EXPERTS = N_GROUPS * EXPERTS_PER_GROUP

LANES = 128
VMEM_LIMIT = 56 * 1024 * 1024


def _cparams(sem):
    return pltpu.CompilerParams(dimension_semantics=sem, vmem_limit_bytes=VMEM_LIMIT)


def _dot(a, b):
    return jnp.dot(a, b, preferred_element_type=F32)


def _dot_nt(a, b):
    return lax.dot_general(a, b, (((1,), (1,)), ((), ())), preferred_element_type=F32)


def _dot_tn(a, b):
    return lax.dot_general(a, b, (((0,), (0,)), ((), ())), preferred_element_type=F32)


def _bdot(a, b):
    return lax.dot_general(a, b, (((2,), (1,)), ((0,), (0,))), preferred_element_type=F32)


def _bdot_nt(a, b):
    return lax.dot_general(a, b, (((2,), (2,)), ((0,), (0,))), preferred_element_type=F32)


def _split_bf16(x):
    hi = x.astype(BF16)
    lo = (x - hi.astype(F32)).astype(BF16)
    return hi, lo


def _mm_body(a_ref, b_ref, o_ref):
    o_ref[...] = _dot(a_ref[...].astype(BF16), b_ref[...]).astype(o_ref.dtype)


def matmul(a, b, out_dtype=F32, tm=512, tn=1024):
    m, k = a.shape
    _, n = b.shape
    tm = min(tm, m)
    tn = min(tn, n)
    assert m % tm == 0 and n % tn == 0
    return pl.pallas_call(
        _mm_body,
        grid=(n // tn, m // tm),
        in_specs=[pl.BlockSpec((tm, k), lambda j, i: (i, 0)),
                  pl.BlockSpec((k, tn), lambda j, i: (0, j))],
        out_specs=pl.BlockSpec((tm, tn), lambda j, i: (i, j)),
        out_shape=jax.ShapeDtypeStruct((m, n), out_dtype),
        compiler_params=_cparams(("parallel", "parallel")),
        name="matmul",
    )(a, b)


def _mm_wcast_body(a_ref, w_ref, o_ref, wb_ref):
    @pl.when(pl.program_id(1) == 0)
    def _():
        wb_ref[...] = w_ref[0].astype(BF16)

    o_ref[...] = _dot(a_ref[...], wb_ref[...]).astype(o_ref.dtype)


def matmul_stacked_w(a, w, layer, out_dtype=F32, tm=512, tn=1024):
    m, k = a.shape
    n = w.shape[2]
    tm = min(tm, m)
    return pl.pallas_call(
        _mm_wcast_body,
        grid=(n // tn, m // tm),
        in_specs=[pl.BlockSpec((tm, k), lambda j, i: (i, 0)),
                  pl.BlockSpec((1, k, tn), lambda j, i: (layer, 0, j), pipeline_mode=pl.Buffered(1))],
        out_specs=pl.BlockSpec((tm, tn), lambda j, i: (i, j)),
        out_shape=jax.ShapeDtypeStruct((m, n), out_dtype),
        scratch_shapes=[pltpu.VMEM((k, tn), BF16)],
        compiler_params=_cparams(("parallel", "arbitrary")),
        name="matmul_stacked_w",
    )(a, w)


def _ln_rows(z, gain, bias):
    mu = jnp.mean(z, axis=-1, keepdims=True)
    zc = z - mu
    var = jnp.mean(zc * zc, axis=-1, keepdims=True)
    return zc * lax.rsqrt(var + LN_EPS) * gain + bias


def _add_ln_body(x_ref, y0_ref, y1_ref, g_ref, b_ref, o_ref, ob_ref):
    z = ALPHA * x_ref[...] + (y0_ref[...].astype(F32) + y1_ref[...].astype(F32))
    y = _ln_rows(z, g_ref[...], b_ref[...])
    o_ref[...] = y
    ob_ref[...] = y.astype(BF16)


def add_ln(x, y0, y1, gain, bias, tm=256):
    m, d = x.shape
    row = pl.BlockSpec((tm, d), lambda i: (i, 0))
    vec = pl.BlockSpec((1, d), lambda i: (0, 0))
    return pl.pallas_call(
        _add_ln_body,
        grid=(m // tm,),
        in_specs=[row, row, row, vec, vec],
        out_specs=[row, row],
        out_shape=[jax.ShapeDtypeStruct((m, d), F32), jax.ShapeDtypeStruct((m, d), BF16)],
        compiler_params=_cparams(("parallel",)),
        name="add_ln",
    )(x, y0, y1, gain.reshape(1, d), bias.reshape(1, d))


def _route(x, wh_ref, wl_ref, b_ref, tri_ref, eid_ref, gate_ref, rank_ref, cnt_ref, base_ref):
    xh, xl = _split_bf16(x)
    wh = wh_ref[...]
    logits = _dot(xh, wh) + (_dot(xl, wh) + _dot(xh, wl_ref[...])) + b_ref[...]
    lane = lax.broadcasted_iota(jnp.int32, logits.shape, 1)
    neg = jnp.float32(-jnp.inf)
    is_g = lane < N_GROUPS
    gl = jnp.where(is_g, logits, neg)
    gmax = jnp.max(gl, axis=-1, keepdims=True)
    g_idx = jnp.min(jnp.where(gl == gmax, lane, LANES), axis=-1, keepdims=True)
    g_p = 1.0 / jnp.sum(jnp.exp(gl - gmax), axis=-1, keepdims=True)
    e_lane = lane - N_GROUPS
    in_grp = (e_lane >= g_idx * EXPERTS_PER_GROUP) & (e_lane < (g_idx + 1) * EXPERTS_PER_GROUP)
    el = jnp.where(in_grp, logits, neg)
    emax = jnp.max(el, axis=-1, keepdims=True)
    pe = jnp.exp(el - emax)
    pe = pe / jnp.sum(pe, axis=-1, keepdims=True)
    v1 = jnp.max(pe, axis=-1, keepdims=True)
    i1 = jnp.min(jnp.where(in_grp & (pe == v1), lane, LANES), axis=-1, keepdims=True)
    rest = in_grp & (lane != i1)
    pe2 = jnp.where(rest, pe, -1.0)
    v2 = jnp.max(pe2, axis=-1, keepdims=True)
    i2 = jnp.min(jnp.where(rest & (pe2 == v2), lane, LANES), axis=-1, keepdims=True)
    den = v1 + v2
    eid_ref[...] = jnp.where(lane == 0, i1 - N_GROUPS, i2 - N_GROUPS)
    gate_ref[...] = jnp.where(lane == 0, g_p * (v1 / den), g_p * (v2 / den))

    @pl.when(pl.program_id(0) == 0)
    def _():
        base_ref[...] = jnp.zeros_like(base_ref)

    hit1 = lane == i1
    hit2 = lane == i2
    oh1 = hit1.astype(BF16)
    oh2 = hit2.astype(BF16)
    tri = tri_ref[...]
    cnt1 = jnp.sum(oh1.astype(F32), axis=0, keepdims=True)
    cnt2 = jnp.sum(oh2.astype(F32), axis=0, keepdims=True)
    base = base_ref[...]
    r1 = jnp.sum(jnp.where(hit1, base + _dot(tri, oh1), 0.0), axis=-1, keepdims=True)
    r2 = jnp.sum(jnp.where(hit2, base + cnt1 + _dot(tri, oh2), 0.0), axis=-1, keepdims=True)
    rank_ref[...] = jnp.where(lane == 0, r1, r2).astype(jnp.int32)
    base_ref[...] = base + cnt1 + cnt2
    cnt_ref[...] = (base + cnt1 + cnt2).astype(jnp.int32)


def _residual_ln_route_body(x_ref, mix_ref, g_ref, b_ref, wh_ref, wl_ref, rb_ref, tri_ref,
                            o_ref, ob_ref, eid_ref, gate_ref, rank_ref, cnt_ref, base_ref):
    y = _ln_rows(ALPHA * x_ref[...] + mix_ref[...], g_ref[...], b_ref[...])
    o_ref[...] = y
    ob_ref[...] = y.astype(BF16)
    _route(y, wh_ref, wl_ref, rb_ref, tri_ref, eid_ref, gate_ref, rank_ref, cnt_ref, base_ref)


def residual_ln_route(x, mix, gain, bias, wg, bg, we, be, tm=256):
    m, d = x.shape
    tm = min(tm, m)
    w = jnp.zeros((d, LANES), F32).at[:, :N_GROUPS].set(wg).at[:, N_GROUPS:N_GROUPS + N_EXPERTS].set(we)
    b = jnp.zeros((1, LANES), F32).at[0, :N_GROUPS].set(bg).at[0, N_GROUPS:N_GROUPS + N_EXPERTS].set(be)
    wh, wl = _split_bf16(w)
    tri = jnp.tril(jnp.ones((tm, tm), BF16), k=-1)
    full = pl.BlockSpec((tm, d), lambda i: (i, 0))
    dvec = pl.BlockSpec((1, d), lambda i: (0, 0))
    row = pl.BlockSpec((tm, LANES), lambda i: (i, 0))
    wspec = pl.BlockSpec((d, LANES), lambda i: (0, 0))
    vec = pl.BlockSpec((1, LANES), lambda i: (0, 0))
    xf, xb, eid, gate, rank, cnt = pl.pallas_call(
        _residual_ln_route_body,
        grid=(m // tm,),
        in_specs=[full, full, dvec, dvec, wspec, wspec, vec, pl.BlockSpec((tm, tm), lambda i: (0, 0))],
        out_specs=[full, full, row, row, row, vec],
        out_shape=[jax.ShapeDtypeStruct((m, d), F32), jax.ShapeDtypeStruct((m, d), BF16),
                   jax.ShapeDtypeStruct((m, LANES), jnp.int32), jax.ShapeDtypeStruct((m, LANES), F32),
                   jax.ShapeDtypeStruct((m, LANES), jnp.int32), jax.ShapeDtypeStruct((1, LANES), jnp.int32)],
        scratch_shapes=[pltpu.VMEM((1, LANES), F32)],
        compiler_params=_cparams(("arbitrary",)),
        name="residual_ln_route",
    )(x, mix, gain.reshape(1, d), bias.reshape(1, d), wh, wl, b, tri)
    return xf, xb, (eid[:, :2], gate[:, :2], rank[:, :2], cnt[0, N_GROUPS:N_GROUPS + N_EXPERTS])


def _expert_body(te_ref, tv_ref, tok_ref, x_hbm, gate_ref, wg_ref, wu_ref, wd_ref, o_ref,
                 wgb_ref, wub_ref, wdb_ref, xbuf_ref, sem_ref):
    i = pl.program_id(0)
    n_tiles = pl.num_programs(0)
    tm = xbuf_ref.shape[1]

    def start_gather(tile, slot):
        def issue(r, carry):
            tok = tok_ref[tile * tm + r]
            pltpu.make_async_copy(x_hbm.at[pl.ds(tok, 1)], xbuf_ref.at[slot, pl.ds(r, 1)],
                                  sem_ref.at[slot]).start()
            return carry

        lax.fori_loop(0, tm, issue, 0, unroll=8)

    @pl.when((i == 0) & (tv_ref[0] > 0))
    def _():
        start_gather(0, 0)

    nxt = jnp.minimum(i + 1, n_tiles - 1)

    @pl.when((i + 1 < n_tiles) & (tv_ref[nxt] > 0))
    def _():
        start_gather(i + 1, (i + 1) % 2)

    @pl.when((i == 0) | (te_ref[i] != te_ref[jnp.maximum(i - 1, 0)]))
    def _():
        wgb_ref[...] = wg_ref[0, 0].astype(BF16)
        wub_ref[...] = wu_ref[0, 0].astype(BF16)
        wdb_ref[...] = wd_ref[0, 0].astype(BF16)

    @pl.when(tv_ref[i] > 0)
    def _():
        slot = i % 2
        pltpu.make_async_copy(x_hbm.at[pl.ds(0, tm)], xbuf_ref.at[slot], sem_ref.at[slot]).wait()
        xs = xbuf_ref[slot].astype(BF16)
        hg = _dot(xs, wgb_ref[...])
        hu = _dot(xs, wub_ref[...])
        act = (hg * jax.nn.sigmoid(hg)) * hu * gate_ref[...]
        o_ref[...] = _dot(act.astype(BF16), wdb_ref[...]).astype(o_ref.dtype)

    @pl.when(tv_ref[i] == 0)
    def _():
        o_ref[...] = jnp.zeros_like(o_ref)


def expert_ffn(x, tok_sorted, gate_sorted, tile_expert, tile_valid, w_gate, w_up, w_down, layer, tm):
    d = x.shape[1]
    p = tok_sorted.shape[0]
    f = w_gate.shape[-1]
    ntiles = p // tm
    grid_spec = pltpu.PrefetchScalarGridSpec(
        num_scalar_prefetch=3,
        grid=(ntiles,),
        in_specs=[pl.BlockSpec(memory_space=pl.ANY),
                  pl.BlockSpec((tm, 1), lambda i, te, tv, tok: (i, 0)),
                  pl.BlockSpec((1, 1, d, f), lambda i, te, tv, tok: (layer, te[i], 0, 0)),
                  pl.BlockSpec((1, 1, d, f), lambda i, te, tv, tok: (layer, te[i], 0, 0)),
                  pl.BlockSpec((1, 1, f, d), lambda i, te, tv, tok: (layer, te[i], 0, 0))],
        out_specs=pl.BlockSpec((tm, d), lambda i, te, tv, tok: (i, 0)),
        scratch_shapes=[pltpu.VMEM((d, f), BF16), pltpu.VMEM((d, f), BF16), pltpu.VMEM((f, d), BF16),
                        pltpu.VMEM((2, tm, d), F32), pltpu.SemaphoreType.DMA((2,))],
    )
    return pl.pallas_call(
        _expert_body,
        grid_spec=grid_spec,
        out_shape=jax.ShapeDtypeStruct((p, d), BF16),
        compiler_params=_cparams(("arbitrary",)),
        name="expert_ffn",
    )(tile_expert, tile_valid, tok_sorted, x, gate_sorted, w_gate, w_up, w_down)


MOE_TILE = 256


def hier_moe(x_f32, routing, w_gate, w_up, w_down, layer):
    n, d = x_f32.shape
    tm = MOE_TILE
    eid, gates, rank, counts = routing
    padded = ((counts + tm - 1) // tm) * tm
    seg_end = jnp.cumsum(padded)
    seg_start = seg_end - padded
    experts = jnp.arange(N_EXPERTS, dtype=jnp.int32)
    pos = (jnp.sum(jnp.where(eid[..., None] == experts, seg_start, 0), axis=-1) + rank).reshape(-1)
    p = 2 * n + N_EXPERTS * tm
    tok = jnp.arange(2 * n, dtype=jnp.int32) // 2
    packed = jnp.stack([tok, lax.bitcast_convert_type(gates.reshape(-1), jnp.int32)], axis=-1)
    packed = jnp.zeros((p, 2), jnp.int32).at[pos].set(packed)
    tok_sorted = packed[:, 0]
    gate_sorted = lax.bitcast_convert_type(packed[:, 1], F32)
    tile_start = jnp.arange(p // tm, dtype=jnp.int32) * tm
    tile_valid = (tile_start < seg_end[-1]).astype(jnp.int32)
    tile_expert = jnp.sum((tile_start[:, None] >= seg_end[None, :]).astype(jnp.int32), axis=1)
    tile_expert = jnp.minimum(tile_expert, N_EXPERTS - 1)
    last_e = jnp.max(jnp.where(tile_valid > 0, tile_expert, 0))
    tile_expert = jnp.where(tile_valid > 0, tile_expert, last_e).astype(jnp.int32)
    ys = expert_ffn(x_f32, tok_sorted, gate_sorted.reshape(p, 1), tile_expert, tile_valid,
                    w_gate, w_up, w_down, layer, tm)
    pos2 = pos.reshape(n, 2)
    return (ys.at[pos2[:, 0]].get(mode="promise_in_bounds"),
            ys.at[pos2[:, 1]].get(mode="promise_in_bounds"))


C_HEAD = 128
HGRN_LEVELS = (1, 2, 4, 8, 16, 32)


def _cumsum_rows(x, row):
    axis = x.ndim - 2
    s = 1
    while s < x.shape[axis]:
        x = x + jnp.where(row >= s, pltpu.roll(x, s, axis=axis), 0.0)
        s *= 2
    return x


def _anchor_rows(b, m):
    n = b.shape[0]
    if m >= 8:
        parts = [jnp.broadcast_to(b[base + m:base + m + 1, :], (2 * m, b.shape[1]))
                 for base in range(0, n, 2 * m)]
        return parts[0] if len(parts) == 1 else jnp.concatenate(parts, axis=0)
    b3 = b.reshape(n // 8, 8, b.shape[1])
    sub = lax.broadcasted_iota(jnp.int32, b3.shape, 1)
    out = None
    for base in range(8 - 2 * m, -1, -2 * m):
        mid = jnp.broadcast_to(b3[:, base + m:base + m + 1, :], b3.shape)
        out = mid if out is None else jnp.where(sub < base + 2 * m, mid, out)
    return out.reshape(n, b.shape[1])


def _hgrn2_body(q_ref, f_ref, i_ref, g_ref, loglb_ref, log1mlb_ref, omlb_ref, gain_ref,
                o_ref, st_ref, *, hb, n_chunk):
    @pl.when(pl.program_id(2) == 0)
    def _():
        st_ref[...] = jnp.zeros_like(st_ref)

    row = lax.broadcasted_iota(jnp.int32, (CHUNK, hb * C_HEAD), 0)
    tt = lax.broadcasted_iota(jnp.int32, (CHUNK, CHUNK), 0)
    ss = lax.broadcasted_iota(jnp.int32, (CHUNK, CHUNK), 1)
    txs = tt ^ ss
    lvl_id = jnp.where(tt == ss, 0, -1)
    for li, m in enumerate(HGRN_LEVELS):
        lvl_id = jnp.where((ss < tt) & (txs >= m) & (txs < 2 * m), li + 1, lvl_id)

    heads = [slice(h * C_HEAD, (h + 1) * C_HEAD) for h in range(hb)]

    def head_scores(qm, km):
        return jnp.stack([_dot_nt(qm[:, hs], km[:, hs]) for hs in heads])

    def chunk_step(c, carry):
        rs = pl.ds(pl.multiple_of(c * CHUNK, CHUNK), CHUNK)
        q = q_ref[rs, :]
        f = f_ref[rs, :]
        v = i_ref[rs, :].astype(BF16)
        g = g_ref[rs, :]
        qs = q * jax.nn.sigmoid(q)
        log_sig = jnp.minimum(f, 0.0) - jnp.log1p(jnp.exp(-jnp.abs(f)))
        y = log1mlb_ref[...] + log_sig
        x = loglb_ref[...]
        log_f = jnp.maximum(x, y) + jnp.log1p(jnp.exp(-jnp.abs(x - y)))
        key = omlb_ref[...] * jax.nn.sigmoid(-f)
        b = _cumsum_rows(log_f, row)
        scores = jnp.where(lvl_id == 0, head_scores(qs.astype(BF16), key.astype(BF16)), 0.0)
        for li, m in enumerate(HGRN_LEVELS):
            upper = (row & m) != 0
            e = jnp.exp(jnp.where(upper, 1.0, -1.0) * (b - _anchor_rows(b, m)))
            xm = (jnp.where(upper, qs, key) * e).astype(BF16)
            scores = jnp.where(lvl_id == li + 1, head_scores(xm, xm), scores)
        scores = scores.astype(BF16)
        q_dec = (qs * jnp.exp(b)).astype(BF16)
        b_last = b[CHUNK - 1:CHUNK, :]
        k_dec = (key * jnp.exp(b_last - b)).astype(BF16)
        s_dec = jnp.exp(b_last)
        outs = []
        for h, hs in enumerate(heads):
            st = st_ref[h]
            o = _dot(scores[h], v[:, hs]) + _dot_nt(q_dec[:, hs], st.astype(BF16))
            st_ref[h] = st * s_dec[:, hs] + _dot_tn(v[:, hs], k_dec[:, hs])
            outs.append(o * lax.rsqrt(jnp.mean(o * o, axis=-1, keepdims=True) + RMS_EPS))
        o = jnp.concatenate(outs, axis=1) * gain_ref[...] * (g * jax.nn.sigmoid(g))
        o_ref[rs, :] = o.astype(o_ref.dtype)
        return carry

    lax.fori_loop(0, n_chunk, chunk_step, 0)


def hgrn2(cols, lb, norm_gain, bsz, seq, hb=16, tc=512):
    n, c4 = cols.shape
    c = c4 // 4
    wl = hb * C_HEAD
    nhb = c // wl
    nt = seq // tc
    lb = lb.astype(F32).reshape(1, c)
    loglb = jnp.log(lb)
    log1mlb = jnp.log1p(-lb)
    omlb = 1.0 - lb

    def col(k):
        return pl.BlockSpec((tc, wl), lambda b, h, t, k=k: (b * nt + t, k * nhb + h))

    vec = pl.BlockSpec((1, wl), lambda b, h, t: (0, h))
    return pl.pallas_call(
        functools.partial(_hgrn2_body, hb=hb, n_chunk=tc // CHUNK),
        grid=(bsz, nhb, nt),
        in_specs=[col(0), col(1), col(2), col(3), vec, vec, vec, vec],
        out_specs=pl.BlockSpec((tc, wl), lambda b, h, t: (b * nt + t, h)),
        out_shape=jax.ShapeDtypeStruct((n, c), BF16),
        scratch_shapes=[pltpu.VMEM((hb, C_HEAD, C_HEAD), F32)],
        compiler_params=_cparams(("parallel", "parallel", "arbitrary")),
        name="hgrn2",
    )(cols, cols, cols, cols, loglb, log1mlb, omlb, norm_gain.reshape(1, c))


B_WIDTH = 2048
B_HEAD = 64
B_HEADS = B_WIDTH // B_HEAD
B_LORA_PAD = 128
B_GATE_LORA = 256
B_COLS_PAD = 3 * B_WIDTH + 2 * B_LORA_PAD + B_GATE_LORA


def _head_pair_ones():
    r = lax.broadcasted_iota(jnp.int32, (LANES, LANES), 0) // B_HEAD
    c = lax.broadcasted_iota(jnp.int32, (LANES, LANES), 1) // B_HEAD
    return (r == c).astype(BF16)


def _head_sums(x, ones):
    outs = []
    for j in range(x.shape[1] // LANES):
        hi, lo = _split_bf16(x[:, j * LANES:(j + 1) * LANES])
        outs.append(_dot(hi, ones) + _dot(lo, ones))
    return outs[0] if len(outs) == 1 else jnp.concatenate(outs, axis=1)


def _dot3(a, wh, wl):
    ah, al = _split_bf16(a)
    return _dot(ah, wh) + (_dot(al, wh) + _dot(ah, wl))


def _softplus(z):
    return jnp.maximum(z, 0.0) + jnp.log1p(jnp.exp(-jnp.abs(z)))


def _rwkv_prep_body(c_ref, p_ref, mu_ref, w0_ref, a0_ref, kk_ref, ka_ref, rk_ref,
                    wuh_ref, wul_ref, auh_ref, aul_ref, gu_ref,
                    r_o, lw_o, k_o, v_o, kk_o, a_o, g_o, bonus_o, *, tiles_per_seq):
    i = pl.program_id(0)
    x = c_ref[...]
    tm = x.shape[0]
    row = lax.broadcasted_iota(jnp.int32, x.shape, 0)
    prev = jnp.where(i % tiles_per_seq == 0, 0.0, p_ref[7:8, :])
    shifted = jnp.where(row == 0, prev, pltpu.roll(x, 1, axis=0))
    mixed = x + (shifted - x) * mu_ref[...]
    w = B_WIDTH
    r = mixed[:, 0:w]
    k = mixed[:, w:2 * w]
    v = mixed[:, 2 * w:3 * w]
    wd = mixed[:, 3 * w:3 * w + B_LORA_PAD]
    ad = mixed[:, 3 * w + B_LORA_PAD:3 * w + 2 * B_LORA_PAD]
    gd = mixed[:, 3 * w + 2 * B_LORA_PAD:]
    wlog = -_softplus(-(w0_ref[...] + _dot3(jnp.tanh(wd), wuh_ref[...], wul_ref[...]))) - 0.5
    a = jax.nn.sigmoid(a0_ref[...] + _dot3(ad, auh_ref[...], aul_ref[...]))
    g = _dot(jax.nn.sigmoid(gd).astype(BF16), gu_ref[...])
    kk = k * kk_ref[...]
    ssq = _head_sums(kk * kk, _head_pair_ones())
    kk = kk / jnp.maximum(jnp.sqrt(ssq), 1e-12)
    k_mod = k * (1.0 + (a - 1.0) * ka_ref[...])
    g_o[...] = g
    bonus_o[...] = _head_sums(r * k_mod * rk_ref[...], _head_pair_ones()) * v
    for out_ref, val in ((r_o, r), (lw_o, -jnp.exp(wlog)), (k_o, k_mod), (v_o, v), (kk_o, kk), (a_o, a)):
        for h in range(B_HEADS):
            out_ref[h] = val[:, h * B_HEAD:(h + 1) * B_HEAD]


def _pad_rows(w, rows):
    return jnp.zeros((rows, w.shape[1]), w.dtype).at[:w.shape[0]].set(w)


def rwkv_prep(cols, mu_pad, w0, w_up, a0, a_up, g_up, k_k, k_a, r_k, seq, tm=128):
    n, cw = cols.shape
    w = B_WIDTH
    wuh, wul = _split_bf16(_pad_rows(w_up, B_LORA_PAD))
    auh, aul = _split_bf16(_pad_rows(a_up, B_LORA_PAD))
    row = pl.BlockSpec((tm, w), lambda i: (i, 0))
    heads = pl.BlockSpec((B_HEADS, tm, B_HEAD), lambda i: (0, i, 0))
    vec = pl.BlockSpec((1, w), lambda i: (0, 0))
    lora = pl.BlockSpec((B_LORA_PAD, w), lambda i: (0, 0))
    outs = pl.pallas_call(
        functools.partial(_rwkv_prep_body, tiles_per_seq=seq // tm),
        grid=(n // tm,),
        in_specs=[pl.BlockSpec((tm, cw), lambda i: (i, 0)),
                  pl.BlockSpec((8, cw), lambda i: (jnp.maximum(i * (tm // 8) - 1, 0), 0)),
                  pl.BlockSpec((1, cw), lambda i: (0, 0)),
                  vec, vec, vec, vec, vec, lora, lora, lora, lora,
                  pl.BlockSpec((B_GATE_LORA, w), lambda i: (0, 0))],
        out_specs=[heads] * 6 + [row] * 2,
        out_shape=[jax.ShapeDtypeStruct((B_HEADS, n, B_HEAD), F32)] * 6 + [jax.ShapeDtypeStruct((n, w), F32)] * 2,
        compiler_params=_cparams(("parallel",)),
        name="rwkv_prep",
    )(cols, cols, mu_pad.reshape(1, cw), w0.reshape(1, w), a0.reshape(1, w), k_k.reshape(1, w),
      k_a.reshape(1, w), r_k.reshape(1, w), wuh, wul, auh, aul, g_up.astype(BF16))
    return outs


def _rwkv_scan_body(r_ref, lw_ref, k_ref, v_ref, kk_ref, a_ref, y_ref, st_ref, qm_ref, ys_ref, *, hb, n_chunk):
    @pl.when(pl.program_id(2) == 0)
    def _():
        st_ref[...] = jnp.zeros_like(st_ref)

    L = CHUNK
    row = lax.broadcasted_iota(jnp.int32, (hb, L, B_HEAD), 1)
    tt = lax.broadcasted_iota(jnp.int32, (L, L), 0)
    ss = lax.broadcasted_iota(jnp.int32, (L, L), 1)
    strict = ss < tt
    incl = ss <= tt
    eye = ss == tt
    eye_f = eye.astype(F32)

    def local_step(c, carry):
        sl = pl.ds(pl.multiple_of(c * L, L), L)
        r = r_ref[:, sl, :]
        lw = lw_ref[:, sl, :]
        k = k_ref[:, sl, :]
        v = v_ref[:, sl, :]
        kk = kk_ref[:, sl, :]
        a = a_ref[:, sl, :]
        lp = _cumsum_rows(lw, row)
        p = jnp.exp(lp)
        inv_p = jnp.exp(-lp)
        at = -kk * jnp.exp(lp - lw)
        bt = kk * a * inv_p
        kt = k * inv_p
        rt = r * p
        p_last = p[:, L - 1:L, :]
        bp = bt * p_last
        kp = kt * p_last
        ar = jnp.concatenate([at, rt], axis=1)
        m_b = _bdot_nt(ar, bt)
        m_k = _bdot_nt(ar, kt)
        nmat = jnp.where(strict, m_b[:, :L], 0.0)
        a_rb = jnp.where(incl, m_b[:, L:], 0.0)
        g = _bdot(jnp.where(strict, m_k[:, :L], 0.0), v)
        tmat = eye_f + nmat
        npow = nmat
        lvl = 2
        while lvl < L:
            npow = _bdot(npow, npow)
            tmat = tmat + _bdot(npow, tmat)
            lvl *= 2
        w = _bdot(tmat, at)
        u = _bdot(tmat, g)
        qe = rt + _bdot(a_rb, w)
        yl = _bdot(a_rb, u) + _bdot(jnp.where(incl, m_k[:, L:], 0.0), v)
        qm_ref[:, c, 0:L, :] = qe
        ys_ref[:, c, 0:L, :] = yl
        for h in range(hb):
            qm_ref[h, c, L:2 * L, :] = _dot_tn(bp[h], w[h]) + jnp.where(eye, p_last[h], 0.0)
            ys_ref[h, c, L:2 * L, :] = _dot_tn(bp[h], u[h]) + _dot_tn(kp[h], v[h])
        return carry

    lax.fori_loop(0, n_chunk, local_step, 0)

    def serial_step(c, carry):
        sl = pl.ds(pl.multiple_of(c * L, L), L)
        out = _bdot(qm_ref[:, c], st_ref[...]) + ys_ref[:, c]
        y_ref[:, sl, :] = out[:, :L]
        st_ref[...] = out[:, L:]
        return carry

    lax.fori_loop(0, n_chunk, serial_step, 0)


def rwkv_scan(r, lw, k, v, kk, a, bsz, seq, hb=32, tc=128):
    nh, n, hd = r.shape
    tc = min(tc, seq)
    nt = seq // tc
    n_chunk = tc // CHUNK
    blk = pl.BlockSpec((hb, tc, hd), lambda b, h, t: (h, b * nt + t, 0))
    return pl.pallas_call(
        functools.partial(_rwkv_scan_body, hb=hb, n_chunk=n_chunk),
        grid=(bsz, nh // hb, nt),
        in_specs=[blk] * 6,
        out_specs=blk,
        out_shape=jax.ShapeDtypeStruct((nh, n, hd), F32),
        scratch_shapes=[pltpu.VMEM((hb, hd, hd), F32),
                        pltpu.VMEM((hb, n_chunk, 2 * CHUNK, hd), F32),
                        pltpu.VMEM((hb, n_chunk, 2 * CHUNK, hd), F32)],
        compiler_params=_cparams(("parallel", "parallel", "arbitrary")),
        name="rwkv_scan",
    )(r, lw, k, v, kk, a)


def _rwkv_post_body(y_ref, g_ref, bonus_ref, lg_ref, lb_ref, o_ref):
    ones = _head_pair_ones()
    y = jnp.concatenate([y_ref[h] for h in range(B_HEADS)], axis=1)
    mean = _head_sums(y, ones) * (1.0 / B_HEAD)
    yc = y - mean
    var = _head_sums(yc * yc, ones) * (1.0 / B_HEAD)
    yn = yc * lax.rsqrt(var + GN_EPS) * lg_ref[...] + lb_ref[...]
    o_ref[...] = ((yn + bonus_ref[...]) * g_ref[...]).astype(o_ref.dtype)


def rwkv_post(y_heads, g, bonus, ln_gain, ln_bias, tm=256):
    n, w = g.shape
    row = pl.BlockSpec((tm, w), lambda i: (i, 0))
    vec = pl.BlockSpec((1, w), lambda i: (0, 0))
    return pl.pallas_call(
        _rwkv_post_body,
        grid=(n // tm,),
        in_specs=[pl.BlockSpec((B_HEADS, tm, B_HEAD), lambda i: (0, i, 0)), row, row, vec, vec],
        out_specs=row,
        out_shape=jax.ShapeDtypeStruct((n, w), BF16),
        compiler_params=_cparams(("parallel",)),
        name="rwkv_post",
    )(y_heads, g, bonus, ln_gain.reshape(1, w), ln_bias.reshape(1, w))


def mixer_rwkv7(cols_b, mu_pad, w0, w_up, a0, a_up, g_up, k_k, k_a, r_k, ln_gain, ln_bias, bsz, seq):
    r, lw, k, v, kk, a, g, bonus = rwkv_prep(cols_b, mu_pad, w0, w_up, a0, a_up, g_up, k_k, k_a,
                                             r_k.reshape(-1), seq)
    y_heads = rwkv_scan(r, lw, k, v, kk, a, bsz, seq)
    return rwkv_post(y_heads, g, bonus, ln_gain, ln_bias)


A_Q_RANK = 768
A_KV_RANK = 512
IDX_DIM = 64
IDX_HEADS = 16
A_HEADS = 16
A_HEAD_DIM = 128
A_COLS_PAD = A_Q_RANK + A_KV_RANK + 2 * LANES
INT_MIN = -2 ** 31


def _dsa_prep_body(c_ref, qn_ref, kn_ref, g_ref, b_ref, q_o, c_o, k_o, w_o):
    x = c_ref[...]
    ql = x[:, :A_Q_RANK]
    q_o[...] = (ql * lax.rsqrt(jnp.mean(ql * ql, axis=-1, keepdims=True) + RMS_EPS) * qn_ref[...]).astype(q_o.dtype)
    kv = x[:, A_Q_RANK:A_Q_RANK + A_KV_RANK]
    c_o[...] = (kv * lax.rsqrt(jnp.mean(kv * kv, axis=-1, keepdims=True) + RMS_EPS) * kn_ref[...]).astype(c_o.dtype)
    o = A_Q_RANK + A_KV_RANK
    ki = x[:, o:o + LANES]
    valid = lax.broadcasted_iota(jnp.int32, ki.shape, 1) < IDX_DIM
    mu = jnp.sum(jnp.where(valid, ki, 0.0), axis=-1, keepdims=True) * (1.0 / IDX_DIM)
    kc = jnp.where(valid, ki - mu, 0.0)
    var = jnp.sum(kc * kc, axis=-1, keepdims=True) * (1.0 / IDX_DIM)
    kn = kc * lax.rsqrt(var + LN_EPS) * g_ref[...] + b_ref[...]
    k_o[...] = kn[:, :IDX_DIM].astype(k_o.dtype)
    w_o[...] = x[:, o + LANES:o + 2 * LANES] * (IDX_HEADS ** -0.5 * IDX_DIM ** -0.5)


def dsa_prep(cols, q_norm, kv_norm, k_gain, k_bias, tm=512):
    n, cw = cols.shape
    pad = jnp.zeros((LANES - IDX_DIM,), F32)
    return pl.pallas_call(
        _dsa_prep_body,
        grid=(n // tm,),
        in_specs=[pl.BlockSpec((tm, cw), lambda i: (i, 0)),
                  pl.BlockSpec((1, A_Q_RANK), lambda i: (0, 0)),
                  pl.BlockSpec((1, A_KV_RANK), lambda i: (0, 0)),
                  pl.BlockSpec((1, LANES), lambda i: (0, 0)),
                  pl.BlockSpec((1, LANES), lambda i: (0, 0))],
        out_specs=[pl.BlockSpec((tm, A_Q_RANK), lambda i: (i, 0)),
                   pl.BlockSpec((tm, A_KV_RANK), lambda i: (i, 0)),
                   pl.BlockSpec((tm, IDX_DIM), lambda i: (i, 0)),
                   pl.BlockSpec((tm, LANES), lambda i: (i, 0))],
        out_shape=[jax.ShapeDtypeStruct((n, A_Q_RANK), BF16), jax.ShapeDtypeStruct((n, A_KV_RANK), BF16),
                   jax.ShapeDtypeStruct((n, IDX_DIM), BF16), jax.ShapeDtypeStruct((n, LANES), F32)],
        compiler_params=_cparams(("parallel",)),
        name="dsa_prep",
    )(cols, q_norm.reshape(1, -1), kv_norm.reshape(1, -1),
      jnp.concatenate([k_gain, pad]).reshape(1, LANES), jnp.concatenate([k_bias, pad]).reshape(1, LANES))


def _head_mm_body(a_ref, w_ref, o_ref, *, scale):
    o_ref[0] = (_dot(a_ref[...], w_ref[0]) * scale).astype(o_ref.dtype)


def head_matmul(a, w, scale, tm=1024):
    n = a.shape[0]
    nh, k, f = w.shape
    tm = min(tm, n)
    return pl.pallas_call(
        functools.partial(_head_mm_body, scale=scale),
        grid=(nh, n // tm),
        in_specs=[pl.BlockSpec((tm, k), lambda h, i: (i, h)),
                  pl.BlockSpec((1, k, f), lambda h, i: (h, 0, 0))],
        out_specs=pl.BlockSpec((1, tm, f), lambda h, i: (h, i, 0)),
        out_shape=jax.ShapeDtypeStruct((nh, n, f), BF16),
        compiler_params=_cparams(("parallel", "parallel")),
        name="head_matmul",
    )(a, w)


DSA_KT = 256
DSA_AT = 1024
DSA_HEAD_GROUP = 4


def _dsa_attend_body(qi_ref, wi_ref, qa_ref, ck_ref, kx_ref, wuv_ref, sl_ref, o_ref,
                     keys_ref, m_ref, l_ref, acc_ref, *, top_k, seq, at_tile):
    j = pl.program_id(1)
    at_n = ((j + 1) * Q_BLOCK + at_tile - 1) // at_tile
    kt_n = at_n * (at_tile // DSA_KT)
    nq = Q_BLOCK
    nh = A_HEADS
    qrow = lax.broadcasted_iota(jnp.int32, (nq, 1), 0)
    qpos = j * Q_BLOCK + qrow
    limit = (qpos // CHUNK + 1) * CHUNK
    lane = lax.broadcasted_iota(jnp.int32, (nq, DSA_KT), 1)
    lane_a = lax.broadcasted_iota(jnp.int32, (nq, at_tile), 1)

    qi = qi_ref[...].reshape(nh * nq, IDX_DIM)
    wi = wi_ref[...]

    def score_tile(kt, carry):
        starts = [pl.multiple_of(kt * at_tile + t * DSA_KT, DSA_KT) for t in range(at_tile // DSA_KT)]
        rels = [_dot_nt(qi, kx_ref[pl.ds(k0, DSA_KT), :]) for k0 in starts]
        for k0, rel in zip(starts, rels):
            score = jnp.sum(jnp.maximum(rel, 0.0).reshape(nh, nq, DSA_KT) * wi, axis=0)
            bits = lax.bitcast_convert_type(score, jnp.int32)
            key = bits ^ ((bits >> 31) & 0x7FFFFFFF)
            keys_ref[:, pl.ds(k0, DSA_KT)] = jnp.where(k0 + lane < limit, key, INT_MIN)
        return carry

    lax.fori_loop(0, at_n, score_tile, 0)

    def count_keys(pred):
        def count_tile(kt, cnt):
            k0 = pl.multiple_of(kt * at_tile, at_tile)
            hit = pred(keys_ref[:, pl.ds(k0, at_tile)], k0).astype(jnp.int32)
            for t in range(at_tile // LANES):
                cnt = cnt + hit[:, t * LANES:(t + 1) * LANES]
            return cnt

        cnt = lax.fori_loop(0, at_n, count_tile, jnp.zeros((nq, LANES), jnp.int32))
        return jnp.sum(cnt, axis=-1, keepdims=True)

    def bit_step(i, carry):
        ans, n_ge = carry
        cand = ans | lax.shift_left(jnp.int32(1), 31 - i)
        cand_s = cand ^ INT_MIN
        total = count_keys(lambda key, k0: key >= cand_s)
        ok = total >= top_k
        return jnp.where(ok, cand, ans), jnp.where(ok, total, n_ge)

    zero = jnp.zeros((nq, 1), jnp.int32)
    ans, n_ge = lax.fori_loop(0, 32, bit_step, (zero, zero))
    thr = ans ^ INT_MIN

    excess = (ans != 0) & (n_ge > top_k)
    idx_bits = max(1, (seq - 1).bit_length())

    def tie_cut():
        need = top_k - count_keys(lambda key, k0: key > thr)

        def idx_step(i, cut):
            cand = cut | lax.shift_left(jnp.int32(1), idx_bits - 1 - i)
            below = count_keys(lambda key, k0: (key == thr) & (k0 + lane_a < cand))
            return jnp.where(below < need, cand, cut)

        return lax.fori_loop(0, idx_bits, idx_step, zero)

    cut = lax.cond(jnp.max(excess.astype(jnp.int32)) > 0, tie_cut, lambda: zero)
    cut = jnp.where(excess, cut, seq)

    m_ref[...] = jnp.full_like(m_ref, -jnp.inf)
    l_ref[...] = jnp.zeros_like(l_ref)
    acc_ref[...] = jnp.zeros_like(acc_ref)
    slopes = sl_ref[...]
    gh = DSA_HEAD_GROUP

    def attend_tile(kt, carry):
        k0 = pl.multiple_of(kt * at_tile, at_tile)
        ck = ck_ref[pl.ds(k0, at_tile), :]
        key = keys_ref[:, pl.ds(k0, at_tile)]
        kidx = k0 + lane_a
        keep = ((key > thr) | ((key == thr) & (kidx <= cut))) & (key > INT_MIN)
        neg_dist = jnp.where(keep, -jnp.abs(qpos - kidx).astype(F32), -jnp.inf)
        def qk(g):
            return _dot_nt(qa_ref[g * gh:(g + 1) * gh].reshape(gh * nq, A_KV_RANK), ck)

        s_next = qk(0)
        for g in range(nh // gh):
            s = s_next
            if g + 1 < nh // gh:
                s_next = qk(g + 1)
            rows = pl.ds(g * gh * nq, gh * nq)
            s = s.reshape(gh, nq, at_tile)
            s = (s + slopes[g * gh:(g + 1) * gh] * neg_dist).reshape(gh * nq, at_tile)
            m_old = m_ref[rows, :]
            m_new = jnp.maximum(m_old, jnp.max(s, axis=-1, keepdims=True))
            m_safe = jnp.where(m_new == -jnp.inf, 0.0, m_new)
            p = jnp.exp(s - m_safe)
            alpha = jnp.exp(m_old - m_safe)
            l_ref[rows, :] = alpha * l_ref[rows, :] + jnp.sum(p, axis=-1, keepdims=True)
            acc_ref[rows, :] = alpha * acc_ref[rows, :] + _dot(p.astype(BF16), ck)
            m_ref[rows, :] = m_new
        return carry

    lax.fori_loop(0, at_n, attend_tile, 0)
    o_lat = (acc_ref[...] / l_ref[...]).astype(BF16).reshape(nh, nq, A_KV_RANK)
    for h in range(nh):
        o_ref[:, h * A_HEAD_DIM:(h + 1) * A_HEAD_DIM] = _dot(o_lat[h], wuv_ref[h]).astype(o_ref.dtype)


def dsa_attend(qi, wi, qa, ckv, kx, w_uv, bsz, seq):
    n = ckv.shape[0]
    nblk = seq // Q_BLOCK
    top_k = min(TOPK_MAX, seq // 4)
    start = 2.0 ** (-8.0 / A_HEADS)
    slopes = jnp.asarray([start ** (h + 1) for h in range(A_HEADS)], F32).reshape(A_HEADS, 1, 1)

    def qblk(last):
        return pl.BlockSpec((A_HEADS, Q_BLOCK, last), lambda b, j: (0, b * nblk + j, 0))

    return pl.pallas_call(
        functools.partial(_dsa_attend_body, top_k=top_k, seq=seq, at_tile=min(DSA_AT, seq)),
        grid=(bsz, nblk),
        in_specs=[qblk(IDX_DIM), qblk(1), qblk(A_KV_RANK),
                  pl.BlockSpec((seq, A_KV_RANK), lambda b, j: (b, 0), pipeline_mode=pl.Buffered(1)),
                  pl.BlockSpec((seq, IDX_DIM), lambda b, j: (b, 0)),
                  pl.BlockSpec((A_HEADS, A_KV_RANK, A_HEAD_DIM), lambda b, j: (0, 0, 0)),
                  pl.BlockSpec((A_HEADS, 1, 1), lambda b, j: (0, 0, 0))],
        out_specs=pl.BlockSpec((Q_BLOCK, A_HEADS * A_HEAD_DIM), lambda b, j: (b * nblk + j, 0)),
        out_shape=jax.ShapeDtypeStruct((n, A_HEADS * A_HEAD_DIM), BF16),
        scratch_shapes=[pltpu.VMEM((Q_BLOCK, seq), jnp.int32),
                        pltpu.VMEM((A_HEADS * Q_BLOCK, 1), F32),
                        pltpu.VMEM((A_HEADS * Q_BLOCK, 1), F32),
                        pltpu.VMEM((A_HEADS * Q_BLOCK, A_KV_RANK), F32)],
        compiler_params=_cparams(("parallel", "arbitrary")),
        name="dsa_attend",
    )(qi, wi, qa, ckv, kx, w_uv, slopes)


def mixer_dsa(cols_a, q_norm, kv_norm, w_uq, w_uk, w_uv, w_idx_q, k_gain, k_bias, bsz, seq):
    n = cols_a.shape[0]
    q_lat, c_kv, k_idx, w_idx = dsa_prep(cols_a, q_norm, kv_norm, k_gain, k_bias)
    q = matmul(q_lat, w_uq.reshape(A_Q_RANK, -1).astype(BF16), out_dtype=BF16)
    qa = head_matmul(q, w_uk.transpose(1, 2, 0).astype(BF16), A_HEAD_DIM ** -0.5)
    qi = matmul(q_lat, w_idx_q.reshape(A_Q_RANK, -1).astype(BF16), out_dtype=BF16)
    qi = qi.reshape(n, IDX_HEADS, IDX_DIM).transpose(1, 0, 2)
    wi = w_idx[:, :IDX_HEADS].T.reshape(IDX_HEADS, n, 1)
    return dsa_attend(qi, wi, qa, c_kv, k_idx, w_uv.transpose(1, 0, 2).astype(BF16), bsz, seq)


A_SPLITS = (A_Q_RANK, A_KV_RANK, IDX_DIM, IDX_HEADS)
A_COLS = sum(A_SPLITS)
DECAY_LORA = 96
AAA_LORA = 96


def _pad_cols(w, sizes, padded):
    out, o = [], 0
    for s, p in zip(sizes, padded):
        piece = w[..., o:o + s]
        if p > s:
            piece = jnp.concatenate([piece, jnp.zeros(w.shape[:-1] + (p - s,), w.dtype)], axis=-1)
        out.append(piece)
        o += s
    return jnp.concatenate(out, axis=-1)


def kernel(x, ln1_gain, ln1_bias, ln2_gain, ln2_bias, w_in_even, a_q_norm, a_kv_norm, a_w_uq, a_w_uk, a_w_uv, a_w_idx_q, a_idx_k_gain, a_idx_k_bias, b_mu, b_w0, b_w_up, b_a0, b_a_up, b_g_up, b_k_k, b_k_a, b_r_k, b_ln_gain, b_ln_bias, w_out_even, w_in_odd, c_lb_logits, c_norm_gain, w_out_odd, router_group, router_group_bias, router_expert, router_expert_bias, moe_w_gate, moe_w_up, moe_w_down):
    bsz, seq, d = x.shape
    n = bsz * seq
    lb_table = jnp.cumsum(jax.nn.softmax(c_lb_logits.astype(F32), axis=0), axis=0)
    lb_table = lb_table - lb_table[:1]
    a_sizes, a_padded = A_SPLITS, (A_Q_RANK, A_KV_RANK, LANES, LANES)
    b_sizes = (3 * B_WIDTH, DECAY_LORA, AAA_LORA, B_GATE_LORA)
    b_padded = (3 * B_WIDTH, B_LORA_PAD, B_LORA_PAD, B_GATE_LORA)
    xf = x.reshape(n, d)
    xb = xf.astype(BF16)
    for layer in range(DEPTH):
        j = layer // 2
        if layer % 2 == 0:
            w_in = w_in_even[j]
            w_a = _pad_cols(w_in[:, :A_COLS], a_sizes, a_padded).astype(BF16)
            w_b = _pad_cols(w_in[:, A_COLS:], b_sizes, b_padded).astype(BF16)
            cols_a = matmul(xb, w_a, tn=512)
            cols_b = matmul(xb, w_b, tn=512)
            y_a = mixer_dsa(cols_a, a_q_norm[j], a_kv_norm[j], a_w_uq[j], a_w_uk[j], a_w_uv[j],
                            a_w_idx_q[j], a_idx_k_gain[j], a_idx_k_bias[j], bsz, seq)
            y_b = mixer_rwkv7(cols_b, _pad_cols(b_mu[j], b_sizes, b_padded), b_w0[j], b_w_up[j], b_a0[j],
                              b_a_up[j], b_g_up[j], b_k_k[j], b_k_a[j], b_r_k[j], b_ln_gain[j], b_ln_bias[j],
                              bsz, seq)
            mixed = jnp.concatenate([y_a, y_b], axis=-1)
            w_out = w_out_even[j]
        else:
            cols = matmul_stacked_w(xb, w_in_odd, j)
            mixed = hgrn2(cols, lb_table[j], c_norm_gain[j], bsz, seq)
            w_out = w_out_odd[j]
        mix = matmul(mixed, w_out.astype(BF16))
        xf, xb, routing = residual_ln_route(xf, mix, ln1_gain[layer], ln1_bias[layer],
                                            router_group[layer], router_group_bias[layer],
                                            router_expert[layer], router_expert_bias[layer])
        y0, y1 = hier_moe(xf, routing, moe_w_gate, moe_w_up, moe_w_down, layer)
        xf, xb = add_ln(xf, y0, y1, ln2_gain[layer], ln2_bias[layer])
    return xf.reshape(bsz, seq, d)
```
